```python
import math
import jax, jax.numpy as jnp
from jax import lax
import numpy as np

D_MODEL = 2048
BATCH = 4
SEQ = 4096
DEPTH = 1

CHUNK = 64
N_META = 16
D_RWKV = 1024
RWKV_HEAD = 64
N_RWKV_HEADS = D_RWKV // RWKV_HEAD
D_DECAY_LORA = 64
D_AAA_LORA = 64
D_CONV = 1024
CONV_GROUP = 64
CONV_WIDTH = 3
RMS_EPS = 1e-6
LNX_EPS = 64e-5

RW_COLS = 4 * D_RWKV + D_DECAY_LORA + D_AAA_LORA
CV_COLS = 4 * D_CONV
IN_COLS = RW_COLS + CV_COLS
D_MIX = D_RWKV + D_CONV

kernel_name = "hymba_rwkv7_shortconv_block"


def _rmsnorm(x, g):
    xf = x.astype(jnp.float32)
    y = xf * lax.rsqrt(jnp.mean(xf * xf, axis=-1, keepdims=True) + RMS_EPS)
    return (y * g.astype(jnp.float32)).astype(x.dtype)


def _rwkv7_recurrence(r, decay, k, v, kk, b):
    bsz, L, H, N = r.shape
    xs = tuple(jnp.moveaxis(t.astype(jnp.float32), 1, 0) for t in (r, decay, k, v, kk, b))

    def step(S, inp):
        r_t, w_t, k_t, v_t, kk_t, b_t = inp
        sa = jnp.einsum('bhvk,bhk->bhv', S, kk_t)
        S = (S * w_t[:, :, None, :]
             - sa[..., None] * b_t[:, :, None, :]
             + v_t[..., None] * k_t[:, :, None, :])
        y_t = jnp.einsum('bhvk,bhk->bhv', S, r_t)
        return S, y_t

    S0 = jnp.zeros((bsz, H, N, N), jnp.float32)
    S, y_meta = lax.scan(step, S0, tuple(t[:N_META] for t in xs))
    real = tuple(t[N_META:].reshape((-1, CHUNK) + t.shape[1:]) for t in xs)

    def chunk_step(S, chunk):
        return lax.scan(step, S, chunk)

    S, y_real = lax.scan(chunk_step, S, real)
    y = jnp.concatenate([y_meta, y_real.reshape((-1,) + y_real.shape[2:])], axis=0)
    return jnp.moveaxis(y, 0, 1).astype(r.dtype)


def _causal_depthwise_conv(u, w):
    C = u.shape[-1]
    return lax.conv_general_dilated(
        u, w[:, None, :].astype(u.dtype),
        window_strides=(1,), padding=[(CONV_WIDTH - 1, 0)],
        dimension_numbers=('NWC', 'WIO', 'NWC'),
        feature_group_count=C)


def setup_inputs(seed: int = 0) -> dict:
    key = jax.random.key(seed)
    ks = jax.random.split(key, 20)
    f32 = jnp.float32
    x = jax.random.normal(ks[0], (BATCH, SEQ, D_MODEL), f32)
    meta_tokens = jax.random.normal(ks[1], (N_META, D_MODEL), f32)
    norm_in_g = 1.0 + 0.05 * jax.random.normal(ks[2], (DEPTH, D_MODEL), f32)
    w_in = jax.random.normal(ks[3], (DEPTH, D_MODEL, IN_COLS), f32) * D_MODEL ** -0.5
    mu_shift = jax.random.uniform(ks[4], (DEPTH, RW_COLS), f32)
    w0 = jax.random.uniform(ks[5], (DEPTH, D_RWKV), f32, minval=-6.0, maxval=1.0)
    w_lora_up = jax.random.normal(ks[6], (DEPTH, D_DECAY_LORA, D_RWKV), f32) * D_DECAY_LORA ** -0.5
    a0 = 0.1 * jax.random.normal(ks[7], (DEPTH, D_RWKV), f32)
    a_lora_up = jax.random.normal(ks[8], (DEPTH, D_AAA_LORA, D_RWKV), f32) * D_AAA_LORA ** -0.5
    k_k = 0.85 + 0.05 * jax.random.normal(ks[9], (DEPTH, D_RWKV), f32)
    k_a = 1.0 + 0.05 * jax.random.normal(ks[10], (DEPTH, D_RWKV), f32)
    r_k = 0.1 * jax.random.normal(ks[11], (DEPTH, N_RWKV_HEADS, RWKV_HEAD), f32)
    lnx_g = 1.0 + 0.05 * jax.random.normal(ks[12], (DEPTH, D_RWKV), f32)
    lnx_b = 0.01 * jax.random.normal(ks[13], (DEPTH, D_RWKV), f32)
    conv_w = jax.random.normal(ks[14], (DEPTH, CONV_WIDTH, D_CONV), f32) * CONV_WIDTH ** -0.5
    w_out = jax.random.normal(ks[15], (DEPTH, D_MIX, D_MODEL), f32) * D_MIX ** -0.5
    norm_f_g = 1.0 + 0.05 * jax.random.normal(ks[16], (D_MODEL,), f32)
    return {"x": x, "meta_tokens": meta_tokens, "norm_in_g": norm_in_g, "w_in": w_in,
            "mu_shift": mu_shift, "w0": w0, "w_lora_up": w_lora_up, "a0": a0,
            "a_lora_up": a_lora_up, "k_k": k_k, "k_a": k_a, "r_k": r_k,
            "lnx_g": lnx_g, "lnx_b": lnx_b, "conv_w": conv_w, "w_out": w_out,
            "norm_f_g": norm_f_g}


def reference(x, meta_tokens, norm_in_g, w_in, mu_shift, w0, w_lora_up, a0, a_lora_up,
              k_k, k_a, r_k, lnx_g, lnx_b, conv_w, w_out, norm_f_g):
    bsz = x.shape[0]
    meta = jnp.broadcast_to(meta_tokens[None].astype(x.dtype), (bsz, N_META, D_MODEL))
    h_res = jnp.concatenate([meta, x], axis=1)
    L = h_res.shape[1]
    H, N = N_RWKV_HEADS, RWKV_HEAD

    for l in range(DEPTH):
        hn = _rmsnorm(h_res, norm_in_g[l])
        P = hn @ w_in[l]
        Pr, Pc = P[..., :RW_COLS], P[..., RW_COLS:]

        prev = jnp.pad(Pr[:, :-1], ((0, 0), (1, 0), (0, 0)))
        Pr = Pr + (prev - Pr) * mu_shift[l]
        r = Pr[..., 0 * D_RWKV:1 * D_RWKV]
        k = Pr[..., 1 * D_RWKV:2 * D_RWKV]
        v = Pr[..., 2 * D_RWKV:3 * D_RWKV]
        g_r = Pr[..., 3 * D_RWKV:4 * D_RWKV]
        wl = Pr[..., 4 * D_RWKV:4 * D_RWKV + D_DECAY_LORA]
        al = Pr[..., 4 * D_RWKV + D_DECAY_LORA:]

        w_raw = -jax.nn.softplus(-(w0[l] + jnp.tanh(wl) @ w_lora_up[l])) - 0.5
        decay = jnp.exp(-jnp.exp(w_raw.astype(jnp.float32)))
        a = jax.nn.sigmoid(a0[l] + al @ a_lora_up[l])

        kk = (k * k_k[l]).reshape(bsz, L, H, N)
        kk = kk / jnp.maximum(jnp.linalg.norm(kk.astype(jnp.float32), axis=-1, keepdims=True), 1e-12).astype(kk.dtype)
        k = k * (1.0 + (a - 1.0) * k_a[l])

        rh = r.reshape(bsz, L, H, N)
        kh = k.reshape(bsz, L, H, N)
        vh = v.reshape(bsz, L, H, N)
        ah = a.reshape(bsz, L, H, N)
        y = _rwkv7_recurrence(rh, decay.reshape(bsz, L, H, N), kh, vh, kk, kk * ah)

        yf = y.astype(jnp.float32)
        mean = jnp.mean(yf, axis=-1, keepdims=True)
        var = jnp.mean(jnp.square(yf - mean), axis=-1, keepdims=True)
        yn = ((yf - mean) * lax.rsqrt(var + LNX_EPS)).reshape(bsz, L, D_RWKV)
        yn = (yn * lnx_g[l] + lnx_b[l]).astype(x.dtype)
        bonus = jnp.sum(rh * kh * r_k[l], axis=-1, keepdims=True) * vh
        y_a = (yn + bonus.reshape(bsz, L, D_RWKV)) * jax.nn.silu(g_r)

        Bg = Pc[..., 0 * D_CONV:1 * D_CONV]
        Cg = Pc[..., 1 * D_CONV:2 * D_CONV]
        hc = Pc[..., 2 * D_CONV:3 * D_CONV]
        g_c = Pc[..., 3 * D_CONV:4 * D_CONV]
        y_b = Bg * _causal_depthwise_conv(Cg * hc, conv_w[l]) * jax.nn.silu(g_c)

        mix = jnp.concatenate([y_a, y_b], axis=-1)
        h_res = h_res + mix @ w_out[l]

    out = _rmsnorm(h_res, norm_f_g)
    return out[:, N_META:]
```

```python
import math

import jax
import jax.numpy as jnp
from jax import lax
from jax.experimental import pallas as pl
from jax.experimental.pallas import tpu as pltpu

F32 = jnp.float32
BF16 = jnp.bfloat16

D_MODEL = 2048
N_META = 16
D_RWKV = 1024
HEAD = 64
N_HEADS = D_RWKV // HEAD
D_LORA = 64
D_CONV = 1024
RW_COLS = 4 * D_RWKV + 2 * D_LORA
IN_COLS = RW_COLS + 4 * D_CONV
RMS_EPS = 1e-6
LNX_EPS = 64e-5

CHUNK = 64
LANES = 128
SUBLANES = 8
N_PAIRS = D_RWKV // LANES
FRONT_PAD = CHUNK - N_META

VMEM_LIMIT = 56 * 1024 * 1024

INPROJ_TM = 640
INPROJ_TN = 1664
OUTPROJ_TM = 256


def _inproj_body(x_ref, g_ref, w_ref, o_ref, hn_ref):
    @pl.when(pl.program_id(1) == 0)
    def _():
        x = x_ref[...]
        ms = jnp.mean(x * x, axis=-1, keepdims=True)
        hn_ref[...] = (x * lax.rsqrt(ms + RMS_EPS) * g_ref[...]).astype(BF16)

    o_ref[...] = jnp.dot(hn_ref[...], w_ref[...], preferred_element_type=F32)


def _inproj(h2d, g, w_bf16):
    rows = h2d.shape[0]
    tm, tn = INPROJ_TM, INPROJ_TN
    assert rows % tm == 0 and IN_COLS % tn == 0
    return pl.pallas_call(
        _inproj_body,
        grid=(rows // tm, IN_COLS // tn),
        in_specs=[
            pl.BlockSpec((tm, D_MODEL), lambda i, j: (i, 0)),
            pl.BlockSpec((1, D_MODEL), lambda i, j: (0, 0)),
            pl.BlockSpec((D_MODEL, tn), lambda i, j: (0, j)),
        ],
        out_specs=pl.BlockSpec((tm, tn), lambda i, j: (i, j)),
        out_shape=jax.ShapeDtypeStruct((rows, IN_COLS), F32),
        scratch_shapes=[pltpu.VMEM((tm, D_MODEL), BF16)],
        compiler_params=pltpu.CompilerParams(
            dimension_semantics=("arbitrary", "arbitrary"),
            vmem_limit_bytes=VMEM_LIMIT),
        name="inproj",
    )(h2d, g, w_bf16)


def _split2(x):
    hi = x.astype(BF16)
    lo = (x - hi.astype(F32)).astype(BF16)
    return [hi, lo]


def _split3(x):
    hi = x.astype(BF16)
    r1 = x - hi.astype(F32)
    mid = r1.astype(BF16)
    lo = (r1 - mid.astype(F32)).astype(BF16)
    return [hi, mid, lo]


def _dot_sum(pairs, ca, cb):
    la, lb = [], []
    for a, b in pairs:
        ah, al = _split2(a)
        bh, bl = _split2(b)
        la += [ah, ah, al]
        lb += [bh, bl, bh]
    return lax.dot_general(jnp.concatenate(la, axis=ca), jnp.concatenate(lb, axis=cb),
                           (((ca,), (cb,)), ((), ())), preferred_element_type=F32)


def _dot_nn(a, b):
    return _dot_sum([(a, b)], 1, 0)


def _dot_nt(a, b):
    return _dot_sum([(a, b)], 1, 1)


def _block_diag(xp):
    lane = lax.broadcasted_iota(jnp.int32, xp.shape, 1)
    first = lane < HEAD
    return jnp.concatenate([jnp.where(first, xp, 0.0), jnp.where(first, 0.0, xp)], axis=0)


def _pair_mm(x, y):
    return _dot_nn(x, _block_diag(y))


def _shift_rows(x, carry, n):
    row = lax.broadcasted_iota(jnp.int32, x.shape, 0)
    out = pltpu.roll(x, n, axis=0)
    for i in range(n):
        src = SUBLANES - n + i
        out = jnp.where(row == i, carry[src:src + 1, :], out)
    return out


def _sigmoid(x):
    return 1.0 / (1.0 + jnp.exp(-x))


def _mixer_body(p_ref, mu_ref, vec_ref, wl_ref, e_ref, et_ref, tri_ref, o_ref,
                carry_rw, carry_u, state, s_at, s_rt, s_kh, s_bh, s_v, s_kd, s_bd, s_pe, s_y):
    @pl.when(pl.program_id(1) == 0)
    def _():
        carry_rw[...] = jnp.zeros_like(carry_rw)
        carry_u[...] = jnp.zeros_like(carry_u)
        state[...] = jnp.zeros_like(state)

    def vec(i):
        return vec_ref[i:i + 1, :]

    w0, a0, k_k, k_a, r_k, lnx_g, lnx_b, cw0, cw1, cw2 = (vec(i) for i in range(10))

    def seg_sum(x):
        s = jnp.dot(jnp.concatenate(_split2(x), axis=1), e_ref[...], preferred_element_type=F32)
        return jnp.dot(jnp.concatenate(_split2(s), axis=1), et_ref[...], preferred_element_type=F32)

    p_rw = p_ref[0, :, :RW_COLS]
    prev = _shift_rows(p_rw, carry_rw[...], 1)
    carry_rw[...] = p_rw[CHUNK - SUBLANES:, :]
    pm = p_rw + (prev - p_rw) * mu_ref[...]
    r = pm[:, 0 * D_RWKV:1 * D_RWKV]
    k = pm[:, 1 * D_RWKV:2 * D_RWKV]
    v = pm[:, 2 * D_RWKV:3 * D_RWKV]
    g_r = pm[:, 3 * D_RWKV:4 * D_RWKV]
    lora_in = pm[:, 4 * D_RWKV:]
    lane = lax.broadcasted_iota(jnp.int32, lora_in.shape, 1)
    lora_in = jnp.where(lane < D_LORA, jnp.tanh(lora_in), lora_in)
    lh, ll = _split2(lora_in)
    lora = jnp.dot(jnp.concatenate([lh, lh, ll], axis=1), wl_ref[...],
                   preferred_element_type=F32)
    logw = (-math.exp(-0.5)) * _sigmoid(w0 + lora[:, :D_RWKV])
    a = _sigmoid(a0 + lora[:, D_RWKV:])

    kk = k * k_k
    kk = kk / jnp.maximum(jnp.sqrt(seg_sum(kk * kk)), 1e-12)
    k = k * (1.0 + (a - 1.0) * k_a)
    beta = kk * a

    cum = jnp.dot(tri_ref[...], jnp.concatenate(_split3(logw), axis=0),
                  preferred_element_type=F32)
    e_in = jnp.exp(cum)
    e_out = jnp.exp(-cum)
    e_ex = jnp.exp(cum - logw)
    p_end = e_in[CHUNK - 1:CHUNK, :]
    a_t = kk * e_ex
    r_t = r * e_in
    k_h = k * e_out
    b_h = beta * e_out
    k_d = k_h * p_end
    b_d = b_h * p_end
    for p in range(N_PAIRS):
        sl = slice(p * LANES, (p + 1) * LANES)
        s_at[p] = a_t[:, sl]
        s_rt[p] = r_t[:, sl]
        s_kh[p] = k_h[:, sl]
        s_bh[p] = b_h[:, sl]
        s_v[p] = v[:, sl]
        s_kd[p] = k_d[:, sl]
        s_bd[p] = b_d[:, sl]
        s_pe[p] = jnp.broadcast_to(p_end[:, sl], (SUBLANES, LANES))

    def pair_step(p, carry):
        at, rt, kh, bh = s_at[p], s_rt[p], s_kh[p], s_bh[p]
        vv, kd, bd = s_v[p], s_kd[p], s_bd[p]
        s0 = state[p]
        h0 = s0.T

        t_i = lax.broadcasted_iota(jnp.int32, (CHUNK, LANES), 0)
        j_i = lax.broadcasted_iota(jnp.int32, (CHUNK, LANES), 1) & (HEAD - 1)
        strict = j_i < t_i
        incl = j_i <= t_i
        same16 = (j_i >> 4) == (t_i >> 4)
        eye = (j_i == t_i).astype(F32)

        lhs = jnp.concatenate([at, rt], axis=0)
        g_b = _dot_nt(lhs, _block_diag(bh))
        g_k = _dot_nt(lhs, _block_diag(kh))
        t_ab = jnp.where(strict, g_b[:CHUNK], 0.0)
        a_rb = jnp.where(incl, g_b[CHUNK:], 0.0)
        t_ak = jnp.where(strict, g_k[:CHUNK], 0.0)
        a_rk = jnp.where(incl, g_k[CHUNK:], 0.0)

        dg = jnp.where(same16, t_ab, 0.0)
        og = t_ab - dg
        d2 = _pair_mm(dg, dg)
        d4 = _pair_mm(d2, d2)
        d8 = _pair_mm(d4, d4)
        x = eye - dg
        x = x + _pair_mm(x, d2)
        x = x + _pair_mm(x, d4)
        x = x + _pair_mm(x, d8)
        tp = _pair_mm(x, og)
        tp2 = _pair_mm(tp, tp)
        z = eye - tp
        z = z + _pair_mm(z, tp2)
        tinv = _pair_mm(z, x)

        vbd = _block_diag(vv)
        u = _pair_mm(tinv, _dot_sum([(at, h0), (t_ak, vbd)], 1, 0))
        s_y[p] = _dot_sum([(rt, h0), (a_rk, vbd), (-a_rb, _block_diag(u))], 1, 0)

        wv_t = jnp.concatenate([vv, u], axis=0).T
        upd = _dot_nn(wv_t, jnp.concatenate([kd, -bd], axis=0))
        row = lax.broadcasted_iota(jnp.int32, (LANES, LANES), 0)
        col = lax.broadcasted_iota(jnp.int32, (LANES, LANES), 1)
        same_head = (row < HEAD) == (col < HEAD)
        state[p] = jnp.where(same_head, s0 * s_pe[p][0:1, :] + upd, 0.0)
        return carry

    lax.fori_loop(0, N_PAIRS, pair_step, 0)
    y = jnp.concatenate([s_y[p] for p in range(N_PAIRS)], axis=1)

    inv_n = 1.0 / HEAD
    mean = seg_sum(y) * inv_n
    yc = y - mean
    var = seg_sum(yc * yc) * inv_n
    yn = yc * lax.rsqrt(var + LNX_EPS) * lnx_g + lnx_b
    bonus = seg_sum(r * k * r_k) * v
    y_a = (yn + bonus) * (g_r * _sigmoid(g_r))

    p_cv = p_ref[0, :, RW_COLS:]
    b_g = p_cv[:, 0 * D_CONV:1 * D_CONV]
    c_g = p_cv[:, 1 * D_CONV:2 * D_CONV]
    h_c = p_cv[:, 2 * D_CONV:3 * D_CONV]
    g_c = p_cv[:, 3 * D_CONV:4 * D_CONV]
    u_c = c_g * h_c
    carry = carry_u[...]
    conv = cw0 * _shift_rows(u_c, carry, 2) + cw1 * _shift_rows(u_c, carry, 1) + cw2 * u_c
    carry_u[...] = u_c[CHUNK - SUBLANES:, :]
    y_b = b_g * conv * (g_c * _sigmoid(g_c))

    o_ref[0, :, :D_RWKV] = y_a.astype(o_ref.dtype)
    o_ref[0, :, D_RWKV:] = y_b.astype(o_ref.dtype)


def _mixer(proj, mu, vecs, w_lora3, seg_e, seg_et, tri3, bsz, n_chunks):
    pair_buf = pltpu.VMEM((N_PAIRS, CHUNK, LANES), F32)
    const = lambda shape: pl.BlockSpec(shape, lambda b, c: (0,) * len(shape))
    return pl.pallas_call(
        _mixer_body,
        grid=(bsz, n_chunks),
        in_specs=[
            pl.BlockSpec((1, CHUNK, IN_COLS), lambda b, c: (b, c, 0)),
            const(mu.shape), const(vecs.shape), const(w_lora3.shape),
            const(seg_e.shape), const(seg_et.shape), const(tri3.shape),
        ],
        out_specs=pl.BlockSpec((1, CHUNK, D_MODEL), lambda b, c: (b, jnp.maximum(c - 1, 0), 0)),
        out_shape=jax.ShapeDtypeStruct((bsz, (n_chunks - 1) * CHUNK, D_MODEL), BF16),
        scratch_shapes=[
            pltpu.VMEM((SUBLANES, RW_COLS), F32),
            pltpu.VMEM((SUBLANES, D_CONV), F32),
            pltpu.VMEM((N_PAIRS, LANES, LANES), F32),
            pair_buf, pair_buf, pair_buf, pair_buf, pair_buf, pair_buf, pair_buf,
            pltpu.VMEM((N_PAIRS, SUBLANES, LANES), F32),
            pair_buf,
        ],
        compiler_params=pltpu.CompilerParams(
            dimension_semantics=("arbitrary", "arbitrary"),
            vmem_limit_bytes=VMEM_LIMIT),
        name="mixer",
    )(proj, mu, vecs, w_lora3, seg_e, seg_et, tri3)


def _outproj_body(mix_ref, x_ref, w_ref, g_ref, o_ref):
    h = x_ref[...] + jnp.dot(mix_ref[...], w_ref[...], preferred_element_type=F32)
    ms = jnp.mean(h * h, axis=-1, keepdims=True)
    o_ref[...] = h * lax.rsqrt(ms + RMS_EPS) * g_ref[...]


def _outproj(mix2d, x2d, w_bf16, g):
    rows = x2d.shape[0]
    tm = OUTPROJ_TM
    assert rows % tm == 0
    return pl.pallas_call(
        _outproj_body,
        grid=(rows // tm,),
        in_specs=[
            pl.BlockSpec((tm, D_MODEL), lambda i: (i, 0)),
            pl.BlockSpec((tm, D_MODEL), lambda i: (i, 0)),
            pl.BlockSpec((D_MODEL, D_MODEL), lambda i: (0, 0)),
            pl.BlockSpec((1, D_MODEL), lambda i: (0, 0)),
        ],
        out_specs=pl.BlockSpec((tm, D_MODEL), lambda i: (i, 0)),
        out_shape=jax.ShapeDtypeStruct((rows, D_MODEL), F32),
        compiler_params=pltpu.CompilerParams(
            dimension_semantics=("arbitrary",),
            vmem_limit_bytes=VMEM_LIMIT),
        name="outproj",
    )(mix2d, x2d, w_bf16, g)


def _stack_split(w):
    hi = w.astype(BF16)
    lo = (w - hi.astype(F32)).astype(BF16)
    return jnp.concatenate([hi, lo, hi], axis=0)


def kernel(x, meta_tokens, norm_in_g, w_in, mu_shift, w0, w_lora_up, a0, a_lora_up, k_k, k_a, r_k, lnx_g, lnx_b, conv_w, w_out, norm_f_g):
    bsz, seq, _ = x.shape
    assert norm_in_g.shape[0] == 1 and seq % CHUNK == 0
    n_chunks = seq // CHUNK + 1
    lp = n_chunks * CHUNK

    front = jnp.concatenate([jnp.zeros((FRONT_PAD, D_MODEL), x.dtype), meta_tokens.astype(x.dtype)], axis=0)
    h_pad = jnp.concatenate([jnp.broadcast_to(front[None], (bsz, CHUNK, D_MODEL)), x], axis=1)
    proj = _inproj(h_pad.reshape(bsz * lp, D_MODEL), norm_in_g[0][None, :], w_in[0].astype(BF16))
    proj = proj.reshape(bsz, lp, IN_COLS)

    w_lora = jnp.zeros((2 * D_LORA, 2 * D_RWKV), F32)
    w_lora = w_lora.at[:D_LORA, :D_RWKV].set(w_lora_up[0]).at[D_LORA:, D_RWKV:].set(a_lora_up[0])
    vecs = jnp.concatenate([
        w0[0][None], a0[0][None], k_k[0][None], k_a[0][None], r_k[0].reshape(1, D_RWKV),
        lnx_g[0][None], lnx_b[0][None], conv_w[0],
        jnp.zeros((16 - 10, D_RWKV), F32)], axis=0)
    head_of_lane = jnp.arange(D_RWKV) // HEAD
    seg = (head_of_lane[:, None] == jnp.arange(LANES)[None, :]).astype(BF16)
    seg_e = jnp.concatenate([seg, seg], axis=0)
    seg_et = jnp.concatenate([seg.T, seg.T], axis=0)
    t_idx = jnp.arange(CHUNK)
    tri = (t_idx[None, :] <= t_idx[:, None]).astype(BF16)
    tri3 = jnp.concatenate([tri, tri, tri], axis=1)

    mix = _mixer(proj, mu_shift, vecs, _stack_split(w_lora), seg_e, seg_et, tri3, bsz, n_chunks)

    out = _outproj(mix.reshape(bsz * seq, D_MODEL), x.reshape(bsz * seq, D_MODEL),
                   w_out[0].astype(BF16), norm_f_g[None, :])
    return out.reshape(bsz, seq, D_MODEL)
```

```python
import math

import jax
import jax.numpy as jnp
from jax import lax
from jax.experimental import pallas as pl
from jax.experimental.pallas import tpu as pltpu

F32 = jnp.float32
BF16 = jnp.bfloat16

D_MODEL = 2048
N_META = 16
D_RWKV = 1024
HEAD = 64
N_HEADS = D_RWKV // HEAD
D_LORA = 64
D_CONV = 1024
RW_COLS = 4 * D_RWKV + 2 * D_LORA
IN_COLS = RW_COLS + 4 * D_CONV
RMS_EPS = 1e-6
LNX_EPS = 64e-5

CHUNK = 64
LANES = 128
SUBLANES = 8
N_PAIRS = D_RWKV // LANES
FRONT_PAD = CHUNK - N_META
PAIR_GROUP = 8

VMEM_LIMIT = 56 * 1024 * 1024

INPROJ_TM = 640
INPROJ_TN = 1664
OUTPROJ_TM = 256


def _inproj_body(x_ref, g_ref, w_ref, o_ref, hn_ref):
    @pl.when(pl.program_id(1) == 0)
    def _():
        x = x_ref[...]
        ms = jnp.mean(x * x, axis=-1, keepdims=True)
        hn_ref[...] = (x * lax.rsqrt(ms + RMS_EPS) * g_ref[...]).astype(BF16)

    o_ref[...] = jnp.dot(hn_ref[...], w_ref[...], preferred_element_type=F32)


def _inproj(h2d, g, w_bf16):
    rows = h2d.shape[0]
    tm, tn = INPROJ_TM, INPROJ_TN
    assert rows % tm == 0 and IN_COLS % tn == 0
    return pl.pallas_call(
        _inproj_body,
        grid=(rows // tm, IN_COLS // tn),
        in_specs=[
            pl.BlockSpec((tm, D_MODEL), lambda i, j: (i, 0)),
            pl.BlockSpec((1, D_MODEL), lambda i, j: (0, 0)),
            pl.BlockSpec((D_MODEL, tn), lambda i, j: (0, j)),
        ],
        out_specs=pl.BlockSpec((tm, tn), lambda i, j: (i, j)),
        out_shape=jax.ShapeDtypeStruct((rows, IN_COLS), F32),
        scratch_shapes=[pltpu.VMEM((tm, D_MODEL), BF16)],
        compiler_params=pltpu.CompilerParams(
            dimension_semantics=("arbitrary", "arbitrary"),
            vmem_limit_bytes=VMEM_LIMIT),
        name="inproj",
    )(h2d, g, w_bf16)


def _split2(x):
    hi = x.astype(BF16)
    lo = (x - hi.astype(F32)).astype(BF16)
    return [hi, lo]


def _split3(x):
    hi = x.astype(BF16)
    r1 = x - hi.astype(F32)
    mid = r1.astype(BF16)
    lo = (r1 - mid.astype(F32)).astype(BF16)
    return [hi, mid, lo]


def _dot_sum(pairs, ca, cb):
    la, lb = [], []
    for a, b in pairs:
        ah, al = _split2(a)
        bh, bl = _split2(b)
        la += [ah, ah, al]
        lb += [bh, bl, bh]
    return lax.dot_general(jnp.concatenate(la, axis=ca), jnp.concatenate(lb, axis=cb),
                           (((ca,), (cb,)), ((), ())), preferred_element_type=F32)


def _dot_nn(a, b):
    return _dot_sum([(a, b)], 1, 0)


def _dot_nt(a, b):
    return _dot_sum([(a, b)], 1, 1)


def _block_diag(xp):
    lane = lax.broadcasted_iota(jnp.int32, xp.shape, 1)
    first = lane < HEAD
    return jnp.concatenate([jnp.where(first, xp, 0.0), jnp.where(first, 0.0, xp)], axis=0)


def _pair_mm(x, y):
    return _dot_nn(x, _block_diag(y))


def _shift_rows(x, carry, n):
    row = lax.broadcasted_iota(jnp.int32, x.shape, 0)
    out = pltpu.roll(x, n, axis=0)
    for i in range(n):
        src = SUBLANES - n + i
        out = jnp.where(row == i, carry[src:src + 1, :], out)
    return out


def _sigmoid(x):
    return 1.0 / (1.0 + jnp.exp(-x))


def _mixer_body(p_ref, mu_ref, vec_ref, wl_ref, e_ref, et_ref, tri_ref, o_ref,
                carry_rw, carry_u, state, s_at, s_rt, s_kh, s_bh, s_v, s_kd, s_bd, s_pe, s_y):
    @pl.when(pl.program_id(1) == 0)
    def _():
        carry_rw[...] = jnp.zeros_like(carry_rw)
        carry_u[...] = jnp.zeros_like(carry_u)
        state[...] = jnp.zeros_like(state)

    def vec(i):
        return vec_ref[i:i + 1, :]

    w0, a0, k_k, k_a, r_k, lnx_g, lnx_b, cw0, cw1, cw2 = (vec(i) for i in range(10))

    def seg_sum(x):
        s = jnp.dot(jnp.concatenate(_split2(x), axis=1), e_ref[...], preferred_element_type=F32)
        return jnp.dot(jnp.concatenate(_split2(s), axis=1), et_ref[...], preferred_element_type=F32)

    p_rw = p_ref[0, :, :RW_COLS]
    prev = _shift_rows(p_rw, carry_rw[...], 1)
    carry_rw[...] = p_rw[CHUNK - SUBLANES:, :]
    pm = p_rw + (prev - p_rw) * mu_ref[...]
    r = pm[:, 0 * D_RWKV:1 * D_RWKV]
    k = pm[:, 1 * D_RWKV:2 * D_RWKV]
    v = pm[:, 2 * D_RWKV:3 * D_RWKV]
    g_r = pm[:, 3 * D_RWKV:4 * D_RWKV]
    lora_in = pm[:, 4 * D_RWKV:]
    lane = lax.broadcasted_iota(jnp.int32, lora_in.shape, 1)
    lora_in = jnp.where(lane < D_LORA, jnp.tanh(lora_in), lora_in)
    lh, ll = _split2(lora_in)
    lora = jnp.dot(jnp.concatenate([lh, lh, ll], axis=1), wl_ref[...],
                   preferred_element_type=F32)
    logw = (-math.exp(-0.5)) * _sigmoid(w0 + lora[:, :D_RWKV])
    a = _sigmoid(a0 + lora[:, D_RWKV:])

    kk = k * k_k
    kk = kk / jnp.maximum(jnp.sqrt(seg_sum(kk * kk)), 1e-12)
    k = k * (1.0 + (a - 1.0) * k_a)
    beta = kk * a

    cum = jnp.dot(tri_ref[...], jnp.concatenate(_split3(logw), axis=0),
                  preferred_element_type=F32)
    e_in = jnp.exp(cum)
    e_out = jnp.exp(-cum)
    e_ex = jnp.exp(cum - logw)
    p_end = e_in[CHUNK - 1:CHUNK, :]
    a_t = kk * e_ex
    r_t = r * e_in
    k_h = k * e_out
    b_h = beta * e_out
    k_d = k_h * p_end
    b_d = b_h * p_end
    for p in range(N_PAIRS):
        sl = slice(p * LANES, (p + 1) * LANES)
        s_at[p] = a_t[:, sl]
        s_rt[p] = r_t[:, sl]
        s_kh[p] = k_h[:, sl]
        s_bh[p] = b_h[:, sl]
        s_v[p] = v[:, sl]
        s_kd[p] = k_d[:, sl]
        s_bd[p] = b_d[:, sl]
        s_pe[p] = jnp.broadcast_to(p_end[:, sl], (SUBLANES, LANES))

    t_i = lax.broadcasted_iota(jnp.int32, (CHUNK, LANES), 0)
    j_i = lax.broadcasted_iota(jnp.int32, (CHUNK, LANES), 1) & (HEAD - 1)
    strict = j_i < t_i
    incl = j_i <= t_i
    same16 = (j_i >> 4) == (t_i >> 4)
    eye = (j_i == t_i).astype(F32)
    row = lax.broadcasted_iota(jnp.int32, (LANES, LANES), 0)
    col = lax.broadcasted_iota(jnp.int32, (LANES, LANES), 1)
    same_head = (row < HEAD) == (col < HEAD)

    def pmm(xs, ys):
        return [_pair_mm(x, y) for x, y in zip(xs, ys)]

    def add(xs, ys):
        return [x + y for x, y in zip(xs, ys)]

    def pair_group(ps):
        at, rt = [s_at[p] for p in ps], [s_rt[p] for p in ps]
        vv = [s_v[p] for p in ps]
        s0 = [state[p] for p in ps]
        h0 = [s.T for s in s0]

        lhs = [jnp.concatenate([a, b], axis=0) for a, b in zip(at, rt)]
        g_b = [_dot_nt(l, _block_diag(s_bh[p])) for l, p in zip(lhs, ps)]
        g_k = [_dot_nt(l, _block_diag(s_kh[p])) for l, p in zip(lhs, ps)]
        t_ab = [jnp.where(strict, g[:CHUNK], 0.0) for g in g_b]
        a_rb = [jnp.where(incl, g[CHUNK:], 0.0) for g in g_b]
        t_ak = [jnp.where(strict, g[:CHUNK], 0.0) for g in g_k]
        a_rk = [jnp.where(incl, g[CHUNK:], 0.0) for g in g_k]

        dg = [jnp.where(same16, t, 0.0) for t in t_ab]
        og = [t - d for t, d in zip(t_ab, dg)]
        vbd = [_block_diag(v_) for v_ in vv]
        rhs = [_dot_sum([(a, h), (t, vb)], 1, 0) for a, h, t, vb in zip(at, h0, t_ak, vbd)]
        d2 = pmm(dg, dg)
        x = [eye - d for d in dg]
        d4 = pmm(d2, d2)
        x = add(x, pmm(x, d2))
        d8 = pmm(d4, d4)
        x = add(x, pmm(x, d4))
        x = add(x, pmm(x, d8))
        tp = pmm(x, og)
        w = pmm(x, rhs)
        tp2 = pmm(tp, tp)
        z = [eye - t for t in tp]
        z = add(z, pmm(z, tp2))
        u = pmm(z, w)

        for i, p in enumerate(ps):
            s_y[p] = _dot_sum([(rt[i], h0[i]), (a_rk[i], vbd[i]), (-a_rb[i], _block_diag(u[i]))], 1, 0)
        for i, p in enumerate(ps):
            wv_t = jnp.concatenate([vv[i], u[i]], axis=0).T
            upd = _dot_nn(wv_t, jnp.concatenate([s_kd[p], -s_bd[p]], axis=0))
            state[p] = jnp.where(same_head, s0[i] * s_pe[p][0:1, :] + upd, 0.0)

    for g in range(0, N_PAIRS, PAIR_GROUP):
        pair_group(range(g, g + PAIR_GROUP))
    y = jnp.concatenate([s_y[p] for p in range(N_PAIRS)], axis=1)

    inv_n = 1.0 / HEAD
    mean = seg_sum(y) * inv_n
    yc = y - mean
    var = seg_sum(yc * yc) * inv_n
    yn = yc * lax.rsqrt(var + LNX_EPS) * lnx_g + lnx_b
    bonus = seg_sum(r * k * r_k) * v
    y_a = (yn + bonus) * (g_r * _sigmoid(g_r))

    p_cv = p_ref[0, :, RW_COLS:]
    b_g = p_cv[:, 0 * D_CONV:1 * D_CONV]
    c_g = p_cv[:, 1 * D_CONV:2 * D_CONV]
    h_c = p_cv[:, 2 * D_CONV:3 * D_CONV]
    g_c = p_cv[:, 3 * D_CONV:4 * D_CONV]
    u_c = c_g * h_c
    carry = carry_u[...]
    conv = cw0 * _shift_rows(u_c, carry, 2) + cw1 * _shift_rows(u_c, carry, 1) + cw2 * u_c
    carry_u[...] = u_c[CHUNK - SUBLANES:, :]
    y_b = b_g * conv * (g_c * _sigmoid(g_c))

    o_ref[0, :, :D_RWKV] = y_a.astype(o_ref.dtype)
    o_ref[0, :, D_RWKV:] = y_b.astype(o_ref.dtype)


def _mixer(proj, mu, vecs, w_lora3, seg_e, seg_et, tri3, bsz, n_chunks):
    pair_buf = pltpu.VMEM((N_PAIRS, CHUNK, LANES), F32)
    const = lambda shape: pl.BlockSpec(shape, lambda b, c: (0,) * len(shape))
    return pl.pallas_call(
        _mixer_body,
        grid=(bsz, n_chunks),
        in_specs=[
            pl.BlockSpec((1, CHUNK, IN_COLS), lambda b, c: (b, c, 0)),
            const(mu.shape), const(vecs.shape), const(w_lora3.shape),
            const(seg_e.shape), const(seg_et.shape), const(tri3.shape),
        ],
        out_specs=pl.BlockSpec((1, CHUNK, D_MODEL), lambda b, c: (b, jnp.maximum(c - 1, 0), 0)),
        out_shape=jax.ShapeDtypeStruct((bsz, (n_chunks - 1) * CHUNK, D_MODEL), BF16),
        scratch_shapes=[
            pltpu.VMEM((SUBLANES, RW_COLS), F32),
            pltpu.VMEM((SUBLANES, D_CONV), F32),
            pltpu.VMEM((N_PAIRS, LANES, LANES), F32),
            pair_buf, pair_buf, pair_buf, pair_buf, pair_buf, pair_buf, pair_buf,
            pltpu.VMEM((N_PAIRS, SUBLANES, LANES), F32),
            pair_buf,
        ],
        compiler_params=pltpu.CompilerParams(
            dimension_semantics=("arbitrary", "arbitrary"),
            vmem_limit_bytes=VMEM_LIMIT),
        name="mixer",
    )(proj, mu, vecs, w_lora3, seg_e, seg_et, tri3)


def _outproj_body(mix_ref, x_ref, w_ref, g_ref, o_ref):
    h = x_ref[...] + jnp.dot(mix_ref[...], w_ref[...], preferred_element_type=F32)
    ms = jnp.mean(h * h, axis=-1, keepdims=True)
    o_ref[...] = h * lax.rsqrt(ms + RMS_EPS) * g_ref[...]


def _outproj(mix2d, x2d, w_bf16, g):
    rows = x2d.shape[0]
    tm = OUTPROJ_TM
    assert rows % tm == 0
    return pl.pallas_call(
        _outproj_body,
        grid=(rows // tm,),
        in_specs=[
            pl.BlockSpec((tm, D_MODEL), lambda i: (i, 0)),
            pl.BlockSpec((tm, D_MODEL), lambda i: (i, 0)),
            pl.BlockSpec((D_MODEL, D_MODEL), lambda i: (0, 0)),
            pl.BlockSpec((1, D_MODEL), lambda i: (0, 0)),
        ],
        out_specs=pl.BlockSpec((tm, D_MODEL), lambda i: (i, 0)),
        out_shape=jax.ShapeDtypeStruct((rows, D_MODEL), F32),
        compiler_params=pltpu.CompilerParams(
            dimension_semantics=("arbitrary",),
            vmem_limit_bytes=VMEM_LIMIT),
        name="outproj",
    )(mix2d, x2d, w_bf16, g)


def _stack_split(w):
    hi = w.astype(BF16)
    lo = (w - hi.astype(F32)).astype(BF16)
    return jnp.concatenate([hi, lo, hi], axis=0)


def kernel(x, meta_tokens, norm_in_g, w_in, mu_shift, w0, w_lora_up, a0, a_lora_up, k_k, k_a, r_k, lnx_g, lnx_b, conv_w, w_out, norm_f_g):
    bsz, seq, _ = x.shape
    assert norm_in_g.shape[0] == 1 and seq % CHUNK == 0
    n_chunks = seq // CHUNK + 1
    lp = n_chunks * CHUNK

    front = jnp.concatenate([jnp.zeros((FRONT_PAD, D_MODEL), x.dtype), meta_tokens.astype(x.dtype)], axis=0)
    h_pad = jnp.concatenate([jnp.broadcast_to(front[None], (bsz, CHUNK, D_MODEL)), x], axis=1)
    proj = _inproj(h_pad.reshape(bsz * lp, D_MODEL), norm_in_g[0][None, :], w_in[0].astype(BF16))
    proj = proj.reshape(bsz, lp, IN_COLS)

    w_lora = jnp.zeros((2 * D_LORA, 2 * D_RWKV), F32)
    w_lora = w_lora.at[:D_LORA, :D_RWKV].set(w_lora_up[0]).at[D_LORA:, D_RWKV:].set(a_lora_up[0])
    vecs = jnp.concatenate([
        w0[0][None], a0[0][None], k_k[0][None], k_a[0][None], r_k[0].reshape(1, D_RWKV),
        lnx_g[0][None], lnx_b[0][None], conv_w[0],
        jnp.zeros((16 - 10, D_RWKV), F32)], axis=0)
    head_of_lane = jnp.arange(D_RWKV) // HEAD
    seg = (head_of_lane[:, None] == jnp.arange(LANES)[None, :]).astype(BF16)
    seg_e = jnp.concatenate([seg, seg], axis=0)
    seg_et = jnp.concatenate([seg.T, seg.T], axis=0)
    t_idx = jnp.arange(CHUNK)
    tri = (t_idx[None, :] <= t_idx[:, None]).astype(BF16)
    tri3 = jnp.concatenate([tri, tri, tri], axis=1)

    mix = _mixer(proj, mu_shift, vecs, _stack_split(w_lora), seg_e, seg_et, tri3, bsz, n_chunks)

    out = _outproj(mix.reshape(bsz * seq, D_MODEL), x.reshape(bsz * seq, D_MODEL),
                   w_out[0].astype(BF16), norm_f_g[None, :])
    return out.reshape(bsz, seq, D_MODEL)
```

```python
import math

import jax
import jax.numpy as jnp
from jax import lax
from jax.experimental import pallas as pl
from jax.experimental.pallas import tpu as pltpu

F32 = jnp.float32
BF16 = jnp.bfloat16

D_MODEL = 2048
N_META = 16
D_RWKV = 1024
HEAD = 64
N_HEADS = D_RWKV // HEAD
D_LORA = 64
D_CONV = 1024
RW_COLS = 4 * D_RWKV + 2 * D_LORA
IN_COLS = RW_COLS + 4 * D_CONV
RMS_EPS = 1e-6
LNX_EPS = 64e-5

CHUNK = 64
LANES = 128
SUBLANES = 8
N_PAIRS = D_RWKV // LANES
FRONT_PAD = CHUNK - N_META
INV_PASSES = 3
OUT_PASSES = 1

VMEM_LIMIT = 56 * 1024 * 1024

INPROJ_TM = 1024
INPROJ_TN = 1664
OUTPROJ_TM = 512


def _rmsnorm_bf16(x, g):
    ms = jnp.mean(x * x, axis=-1, keepdims=True)
    return (x * lax.rsqrt(ms + RMS_EPS) * g).astype(BF16)


def _inproj_body(x_ref, g_ref, w_ref, o_ref, hn_ref):
    @pl.when(pl.program_id(2) == 0)
    def _():
        hn_ref[...] = _rmsnorm_bf16(x_ref[0], g_ref[...])

    o_ref[0] = jnp.dot(hn_ref[...], w_ref[...], preferred_element_type=F32)


def _inproj(x, g, w_bf16):
    bsz, seq, _ = x.shape
    tm, tn = INPROJ_TM, INPROJ_TN
    assert seq % tm == 0 and IN_COLS % tn == 0
    return pl.pallas_call(
        _inproj_body,
        grid=(bsz, seq // tm, IN_COLS // tn),
        in_specs=[
            pl.BlockSpec((1, tm, D_MODEL), lambda b, i, j: (b, i, 0)),
            pl.BlockSpec((1, D_MODEL), lambda b, i, j: (0, 0)),
            pl.BlockSpec((D_MODEL, tn), lambda b, i, j: (0, j)),
        ],
        out_specs=pl.BlockSpec((1, tm, tn), lambda b, i, j: (b, i, j)),
        out_shape=jax.ShapeDtypeStruct((bsz, seq + CHUNK, IN_COLS), F32),
        scratch_shapes=[pltpu.VMEM((tm, D_MODEL), BF16)],
        compiler_params=pltpu.CompilerParams(
            dimension_semantics=("arbitrary", "arbitrary", "arbitrary"),
            vmem_limit_bytes=VMEM_LIMIT),
        name="inproj",
    )(x, g, w_bf16)


def _inproj_meta_body(x_ref, g_ref, w_ref, p_ref, o_ref):
    del p_ref
    res = jnp.dot(_rmsnorm_bf16(x_ref[...], g_ref[...]), w_ref[...], preferred_element_type=F32)
    o_ref[...] = jnp.broadcast_to(res[None], o_ref.shape)


def _inproj_meta(front, g, w_bf16, proj):
    bsz, rows, _ = proj.shape
    tn = INPROJ_TN
    return pl.pallas_call(
        _inproj_meta_body,
        grid=(IN_COLS // tn,),
        in_specs=[
            pl.BlockSpec((CHUNK, D_MODEL), lambda j: (0, 0)),
            pl.BlockSpec((1, D_MODEL), lambda j: (0, 0)),
            pl.BlockSpec((D_MODEL, tn), lambda j: (0, j)),
            pl.BlockSpec(memory_space=pl.ANY),
        ],
        out_specs=pl.BlockSpec((bsz, CHUNK, tn), lambda j: (0, rows // CHUNK - 1, j)),
        out_shape=jax.ShapeDtypeStruct(proj.shape, proj.dtype),
        input_output_aliases={3: 0},
        compiler_params=pltpu.CompilerParams(
            dimension_semantics=("arbitrary",),
            vmem_limit_bytes=VMEM_LIMIT),
        name="inproj_meta",
    )(front, g, w_bf16, proj)


class _Split:
    def __init__(self, x):
        self.x = x
        self._hi = None
        self._lo = None

    @property
    def hi(self):
        if self._hi is None:
            self._hi = self.x.astype(BF16)
        return self._hi

    @property
    def lo(self):
        if self._lo is None:
            self._lo = (self.x - self.hi.astype(F32)).astype(BF16)
        return self._lo


class _BlockDiag:
    def __init__(self, y):
        self._y = y
        self._hi = None
        self._lo = None

    @staticmethod
    def _expand(yp):
        lane = lax.broadcasted_iota(jnp.int32, yp.shape, 1)
        first = lane < HEAD
        zero = jnp.zeros_like(yp)
        return jnp.concatenate([jnp.where(first, yp, zero), jnp.where(first, zero, yp)], axis=0)

    @property
    def hi(self):
        if self._hi is None:
            self._hi = self._expand(self._y.hi)
        return self._hi

    @property
    def lo(self):
        if self._lo is None:
            self._lo = self._expand(self._y.lo)
        return self._lo


def _mm(terms, passes, nt=False):
    la, lb = [], []
    for a, b in terms:
        if passes == 1:
            la += [a.hi]
            lb += [b.hi]
        else:
            la += [a.hi, a.hi, a.lo]
            lb += [b.hi, b.lo, b.hi]
    cb = 1 if nt else 0
    return lax.dot_general(jnp.concatenate(la, axis=1), jnp.concatenate(lb, axis=cb),
                           (((1,), (cb,)), ((), ())), preferred_element_type=F32)


def _split3(x):
    hi = x.astype(BF16)
    r1 = x - hi.astype(F32)
    mid = r1.astype(BF16)
    lo = (r1 - mid.astype(F32)).astype(BF16)
    return [hi, mid, lo]


def _shift_rows(x, carry, n):
    row = lax.broadcasted_iota(jnp.int32, x.shape, 0)
    out = pltpu.roll(x, n, axis=0)
    for i in range(n):
        src = SUBLANES - n + i
        out = jnp.where(row == i, carry[src:src + 1, :], out)
    return out


def _sigmoid(x):
    return 1.0 / (1.0 + jnp.exp(-x))


def _mixer_body(p_ref, mu_ref, vec_ref, wl_ref, ones_ref, tri_ref, o_ref,
                carry_rw, carry_u, state, s_at, s_rt, s_kh, s_bh, s_v, s_kd, s_bd, s_pe, s_y):
    @pl.when(pl.program_id(1) == 0)
    def _():
        carry_rw[...] = jnp.zeros_like(carry_rw)
        carry_u[...] = jnp.zeros_like(carry_u)
        state[...] = jnp.zeros_like(state)

    def vec(i):
        return vec_ref[i:i + 1, :]

    w0, a0, k_k, k_a, r_k, lnx_g, lnx_b, cw0, cw1, cw2 = (vec(i) for i in range(10))

    def seg_sums(xs):
        tiles = [x[:, p * LANES:(p + 1) * LANES] for x in xs for p in range(N_PAIRS)]
        t = _Split(jnp.concatenate(tiles, axis=0))
        s = jnp.dot(jnp.concatenate([t.hi, t.lo], axis=1), ones_ref[...], preferred_element_type=F32)
        out = []
        for i in range(len(xs)):
            rows = [s[(i * N_PAIRS + p) * CHUNK:(i * N_PAIRS + p + 1) * CHUNK] for p in range(N_PAIRS)]
            out.append(jnp.concatenate(rows, axis=1))
        return out

    p_rw = p_ref[0, :, :RW_COLS]
    prev = _shift_rows(p_rw, carry_rw[...], 1)
    carry_rw[...] = p_rw[CHUNK - SUBLANES:, :]
    pm = p_rw + (prev - p_rw) * mu_ref[...]
    r = pm[:, 0 * D_RWKV:1 * D_RWKV]
    k = pm[:, 1 * D_RWKV:2 * D_RWKV]
    v = pm[:, 2 * D_RWKV:3 * D_RWKV]
    g_r = pm[:, 3 * D_RWKV:4 * D_RWKV]
    lora_in = pm[:, 4 * D_RWKV:]
    lane = lax.broadcasted_iota(jnp.int32, lora_in.shape, 1)
    lora_in = _Split(jnp.where(lane < D_LORA, jnp.tanh(lora_in), lora_in))
    lora = jnp.dot(jnp.concatenate([lora_in.hi, lora_in.hi, lora_in.lo], axis=1), wl_ref[...],
                   preferred_element_type=F32)
    logw = (-math.exp(-0.5)) * _sigmoid(w0 + lora[:, :D_RWKV])
    a = _sigmoid(a0 + lora[:, D_RWKV:])

    kk = k * k_k
    k = k * (1.0 + (a - 1.0) * k_a)
    kk_sq, rk_sum = seg_sums([kk * kk, r * k * r_k])
    kk = kk / jnp.maximum(jnp.sqrt(kk_sq), 1e-12)
    beta = kk * a

    cum = jnp.dot(tri_ref[...], jnp.concatenate(_split3(logw), axis=0),
                  preferred_element_type=F32)
    e_in = jnp.exp(cum)
    e_out = jnp.exp(-cum)
    e_ex = jnp.exp(cum - logw)
    p_end = e_in[CHUNK - 1:CHUNK, :]
    a_t = kk * e_ex
    r_t = r * e_in
    k_h = k * e_out
    b_h = beta * e_out
    k_d = k_h * p_end
    b_d = b_h * p_end
    for p in range(N_PAIRS):
        sl = slice(p * LANES, (p + 1) * LANES)
        s_at[p] = a_t[:, sl]
        s_rt[p] = r_t[:, sl]
        s_kh[p] = k_h[:, sl]
        s_bh[p] = b_h[:, sl]
        s_v[p] = v[:, sl]
        s_kd[p] = k_d[:, sl]
        s_bd[p] = b_d[:, sl]
        s_pe[p] = jnp.broadcast_to(p_end[:, sl], (SUBLANES, LANES))

    t_i = lax.broadcasted_iota(jnp.int32, (CHUNK, LANES), 0)
    j_i = lax.broadcasted_iota(jnp.int32, (CHUNK, LANES), 1) & (HEAD - 1)
    strict = j_i < t_i
    incl = j_i <= t_i
    same16 = (j_i >> 4) == (t_i >> 4)
    eye = (j_i == t_i).astype(F32)
    row = lax.broadcasted_iota(jnp.int32, (LANES, LANES), 0)
    col = lax.broadcasted_iota(jnp.int32, (LANES, LANES), 1)
    same_head = (row < HEAD) == (col < HEAD)
    ps = range(N_PAIRS)

    def splits(xs):
        return [_Split(x) for x in xs]

    def pmm(xs, ys):
        return [_mm([(x, _BlockDiag(y))], INV_PASSES) for x, y in zip(xs, ys)]

    def plus(xs, ys):
        return splits([x.x + y for x, y in zip(xs, ys)])

    at, rt = splits([s_at[p] for p in ps]), splits([s_rt[p] for p in ps])
    vv = [s_v[p] for p in ps]
    vbd = [_BlockDiag(_Split(v_)) for v_ in vv]
    s0 = [state[p] for p in ps]
    h0 = splits([s.T for s in s0])

    lhs = splits([jnp.concatenate([a_.x, b_.x], axis=0) for a_, b_ in zip(at, rt)])
    g_b = [_mm([(l, _BlockDiag(_Split(s_bh[p])))], OUT_PASSES, nt=True) for l, p in zip(lhs, ps)]
    g_k = [_mm([(l, _BlockDiag(_Split(s_kh[p])))], OUT_PASSES, nt=True) for l, p in zip(lhs, ps)]
    t_ab = [jnp.where(strict, g[:CHUNK], 0.0) for g in g_b]
    a_rb = [jnp.where(incl, g[CHUNK:], 0.0) for g in g_b]
    t_ak = [jnp.where(strict, g[:CHUNK], 0.0) for g in g_k]
    a_rk = [jnp.where(incl, g[CHUNK:], 0.0) for g in g_k]

    dg = splits([jnp.where(same16, t, 0.0) for t in t_ab])
    og = splits([t - d.x for t, d in zip(t_ab, dg)])
    rhs = splits([_mm([(a_, h), (_Split(t), vb)], OUT_PASSES) for a_, h, t, vb in zip(at, h0, t_ak, vbd)])
    d2 = splits(pmm(dg, dg))
    x = splits([eye - d.x for d in dg])
    d4 = splits(pmm(d2, d2))
    x = plus(x, pmm(x, d2))
    d8 = splits(pmm(d4, d4))
    x = plus(x, pmm(x, d4))
    x = plus(x, pmm(x, d8))
    tp = splits(pmm(x, og))
    w = splits(pmm(x, rhs))
    tp2 = splits(pmm(tp, tp))
    z = splits([eye - t.x for t in tp])
    z = plus(z, pmm(z, tp2))
    u = splits(pmm(z, w))

    for p in ps:
        s_y[p] = _mm([(rt[p], h0[p]), (_Split(a_rk[p]), vbd[p]), (_Split(-a_rb[p]), _BlockDiag(u[p]))],
                     OUT_PASSES)
    for p in ps:
        wv_t = _Split(jnp.concatenate([vv[p], u[p].x], axis=0).T)
        upd = _mm([(wv_t, _Split(jnp.concatenate([s_kd[p], -s_bd[p]], axis=0)))], OUT_PASSES)
        state[p] = jnp.where(same_head, s0[p] * s_pe[p][0:1, :] + upd, 0.0)
    y = jnp.concatenate([s_y[p] for p in ps], axis=1)

    inv_n = 1.0 / HEAD
    mean = seg_sums([y])[0] * inv_n
    yc = y - mean
    var = seg_sums([yc * yc])[0] * inv_n
    yn = yc * lax.rsqrt(var + LNX_EPS) * lnx_g + lnx_b
    y_a = (yn + rk_sum * v) * (g_r * _sigmoid(g_r))

    p_cv = p_ref[0, :, RW_COLS:]
    b_g = p_cv[:, 0 * D_CONV:1 * D_CONV]
    c_g = p_cv[:, 1 * D_CONV:2 * D_CONV]
    h_c = p_cv[:, 2 * D_CONV:3 * D_CONV]
    g_c = p_cv[:, 3 * D_CONV:4 * D_CONV]
    u_c = c_g * h_c
    carry = carry_u[...]
    conv = cw0 * _shift_rows(u_c, carry, 2) + cw1 * _shift_rows(u_c, carry, 1) + cw2 * u_c
    carry_u[...] = u_c[CHUNK - SUBLANES:, :]
    y_b = b_g * conv * (g_c * _sigmoid(g_c))

    o_ref[0, :, :D_RWKV] = y_a.astype(o_ref.dtype)
    o_ref[0, :, D_RWKV:] = y_b.astype(o_ref.dtype)


def _mixer(proj, mu, vecs, w_lora3, seg_ones, tri3):
    bsz, rows, _ = proj.shape
    n_chunks = rows // CHUNK
    seq = rows - CHUNK
    pair_buf = pltpu.VMEM((N_PAIRS, CHUNK, LANES), F32)
    const = lambda shape: pl.BlockSpec(shape, lambda b, c: (0,) * len(shape))
    return pl.pallas_call(
        _mixer_body,
        grid=(bsz, n_chunks),
        in_specs=[
            pl.BlockSpec((1, CHUNK, IN_COLS), lambda b, c: (b, jnp.where(c == 0, n_chunks - 1, c - 1), 0)),
            const(mu.shape), const(vecs.shape), const(w_lora3.shape),
            const(seg_ones.shape), const(tri3.shape),
        ],
        out_specs=pl.BlockSpec((1, CHUNK, D_MODEL), lambda b, c: (b, jnp.maximum(c - 1, 0), 0)),
        out_shape=jax.ShapeDtypeStruct((bsz, seq, D_MODEL), BF16),
        scratch_shapes=[
            pltpu.VMEM((SUBLANES, RW_COLS), F32),
            pltpu.VMEM((SUBLANES, D_CONV), F32),
            pltpu.VMEM((N_PAIRS, LANES, LANES), F32),
            pair_buf, pair_buf, pair_buf, pair_buf, pair_buf, pair_buf, pair_buf,
            pltpu.VMEM((N_PAIRS, SUBLANES, LANES), F32),
            pair_buf,
        ],
        compiler_params=pltpu.CompilerParams(
            dimension_semantics=("arbitrary", "arbitrary"),
            vmem_limit_bytes=VMEM_LIMIT),
        name="mixer",
    )(proj, mu, vecs, w_lora3, seg_ones, tri3)


def _outproj_body(mix_ref, x_ref, w_ref, g_ref, o_ref):
    h = x_ref[...] + jnp.dot(mix_ref[...], w_ref[...], preferred_element_type=F32)
    ms = jnp.mean(h * h, axis=-1, keepdims=True)
    o_ref[...] = h * lax.rsqrt(ms + RMS_EPS) * g_ref[...]


def _outproj(mix2d, x2d, w_bf16, g):
    rows = x2d.shape[0]
    tm = OUTPROJ_TM
    assert rows % tm == 0
    return pl.pallas_call(
        _outproj_body,
        grid=(rows // tm,),
        in_specs=[
            pl.BlockSpec((tm, D_MODEL), lambda i: (i, 0)),
            pl.BlockSpec((tm, D_MODEL), lambda i: (i, 0)),
            pl.BlockSpec((D_MODEL, D_MODEL), lambda i: (0, 0)),
            pl.BlockSpec((1, D_MODEL), lambda i: (0, 0)),
        ],
        out_specs=pl.BlockSpec((tm, D_MODEL), lambda i: (i, 0)),
        out_shape=jax.ShapeDtypeStruct((rows, D_MODEL), F32),
        compiler_params=pltpu.CompilerParams(
            dimension_semantics=("arbitrary",),
            vmem_limit_bytes=VMEM_LIMIT),
        name="outproj",
    )(mix2d, x2d, w_bf16, g)


def _stack_split(w):
    hi = w.astype(BF16)
    lo = (w - hi.astype(F32)).astype(BF16)
    return jnp.concatenate([hi, lo, hi], axis=0)


def kernel(x, meta_tokens, norm_in_g, w_in, mu_shift, w0, w_lora_up, a0, a_lora_up, k_k, k_a, r_k, lnx_g, lnx_b, conv_w, w_out, norm_f_g):
    bsz, seq, _ = x.shape
    assert norm_in_g.shape[0] == 1 and seq % CHUNK == 0
    x2d = x.reshape(bsz * seq, D_MODEL)

    front = jnp.concatenate([jnp.zeros((FRONT_PAD, D_MODEL), x.dtype), meta_tokens.astype(x.dtype)], axis=0)
    g_in = norm_in_g[0][None, :]
    w_in_bf16 = w_in[0].astype(BF16)
    proj = _inproj_meta(front, g_in, w_in_bf16, _inproj(x, g_in, w_in_bf16))

    w_lora = jnp.zeros((2 * D_LORA, 2 * D_RWKV), F32)
    w_lora = w_lora.at[:D_LORA, :D_RWKV].set(w_lora_up[0]).at[D_LORA:, D_RWKV:].set(a_lora_up[0])
    vecs = jnp.concatenate([
        w0[0][None], a0[0][None], k_k[0][None], k_a[0][None], r_k[0].reshape(1, D_RWKV),
        lnx_g[0][None], lnx_b[0][None], conv_w[0],
        jnp.zeros((16 - 10, D_RWKV), F32)], axis=0)
    lane_head = jnp.arange(LANES) // HEAD
    ones_bd = (lane_head[:, None] == lane_head[None, :]).astype(BF16)
    seg_ones = jnp.concatenate([ones_bd, ones_bd], axis=0)
    t_idx = jnp.arange(CHUNK)
    tri = (t_idx[None, :] <= t_idx[:, None]).astype(BF16)
    tri3 = jnp.concatenate([tri, tri, tri], axis=1)

    mix = _mixer(proj, mu_shift, vecs, _stack_split(w_lora), seg_ones, tri3)

    out = _outproj(mix.reshape(bsz * seq, D_MODEL), x2d, w_out[0].astype(BF16), norm_f_g[None, :])
    return out.reshape(bsz, seq, D_MODEL)
```

```python
import math

import jax
import jax.numpy as jnp
from jax import lax
from jax.experimental import pallas as pl
from jax.experimental.pallas import tpu as pltpu

F32 = jnp.float32
BF16 = jnp.bfloat16

D_MODEL = 2048
N_META = 16
D_RWKV = 1024
HEAD = 64
N_HEADS = D_RWKV // HEAD
D_LORA = 64
D_CONV = 1024
RW_COLS = 4 * D_RWKV + 2 * D_LORA
IN_COLS = RW_COLS + 4 * D_CONV
RMS_EPS = 1e-6
LNX_EPS = 64e-5

CHUNK = 64
LANES = 128
SUBLANES = 8
N_PAIRS = D_RWKV // LANES
FRONT_PAD = CHUNK - N_META

VMEM_LIMIT = 56 * 1024 * 1024

INPROJ_TM = 1024
INPROJ_TN = 1664
OUTPROJ_TM = 512


def _rmsnorm_bf16(x, g):
    ms = jnp.mean(x * x, axis=-1, keepdims=True)
    return (x * lax.rsqrt(ms + RMS_EPS) * g).astype(BF16)


def _inproj_body(x_ref, g_ref, w_ref, o_ref, hn_ref):
    @pl.when(pl.program_id(2) == 0)
    def _():
        hn_ref[...] = _rmsnorm_bf16(x_ref[0], g_ref[...])

    o_ref[0] = jnp.dot(hn_ref[...], w_ref[...], preferred_element_type=F32)


def _inproj(x, g, w_bf16):
    bsz, seq, _ = x.shape
    tm, tn = INPROJ_TM, INPROJ_TN
    assert seq % tm == 0 and IN_COLS % tn == 0
    return pl.pallas_call(
        _inproj_body,
        grid=(bsz, seq // tm, IN_COLS // tn),
        in_specs=[
            pl.BlockSpec((1, tm, D_MODEL), lambda b, i, j: (b, i, 0)),
            pl.BlockSpec((1, D_MODEL), lambda b, i, j: (0, 0)),
            pl.BlockSpec((D_MODEL, tn), lambda b, i, j: (0, j)),
        ],
        out_specs=pl.BlockSpec((1, tm, tn), lambda b, i, j: (b, i, j)),
        out_shape=jax.ShapeDtypeStruct((bsz, seq + CHUNK, IN_COLS), F32),
        scratch_shapes=[pltpu.VMEM((tm, D_MODEL), BF16)],
        compiler_params=pltpu.CompilerParams(
            dimension_semantics=("arbitrary", "arbitrary", "arbitrary"),
            vmem_limit_bytes=VMEM_LIMIT),
        name="inproj",
    )(x, g, w_bf16)


def _inproj_meta_body(x_ref, g_ref, w_ref, p_ref, o_ref):
    del p_ref
    res = jnp.dot(_rmsnorm_bf16(x_ref[...], g_ref[...]), w_ref[...], preferred_element_type=F32)
    o_ref[...] = jnp.broadcast_to(res[None], o_ref.shape)


def _inproj_meta(front, g, w_bf16, proj):
    bsz, rows, _ = proj.shape
    tn = INPROJ_TN
    return pl.pallas_call(
        _inproj_meta_body,
        grid=(IN_COLS // tn,),
        in_specs=[
            pl.BlockSpec((CHUNK, D_MODEL), lambda j: (0, 0)),
            pl.BlockSpec((1, D_MODEL), lambda j: (0, 0)),
            pl.BlockSpec((D_MODEL, tn), lambda j: (0, j)),
            pl.BlockSpec(memory_space=pl.ANY),
        ],
        out_specs=pl.BlockSpec((bsz, CHUNK, tn), lambda j: (0, rows // CHUNK - 1, j)),
        out_shape=jax.ShapeDtypeStruct(proj.shape, proj.dtype),
        input_output_aliases={3: 0},
        compiler_params=pltpu.CompilerParams(
            dimension_semantics=("arbitrary",),
            vmem_limit_bytes=VMEM_LIMIT),
        name="inproj_meta",
    )(front, g, w_bf16, proj)


class _Split:
    def __init__(self, x):
        self.x = x
        self._hi = None
        self._lo = None

    @property
    def hi(self):
        if self._hi is None:
            self._hi = self.x.astype(BF16)
        return self._hi

    @property
    def lo(self):
        if self._lo is None:
            self._lo = (self.x - self.hi.astype(F32)).astype(BF16)
        return self._lo


class _BlockDiag:
    def __init__(self, y):
        self._y = y
        self._hi = None
        self._lo = None

    @staticmethod
    def _expand(yp):
        lane = lax.broadcasted_iota(jnp.int32, yp.shape, 1)
        first = lane < HEAD
        zero = jnp.zeros_like(yp)
        return jnp.concatenate([jnp.where(first, yp, zero), jnp.where(first, zero, yp)], axis=0)

    @property
    def hi(self):
        if self._hi is None:
            self._hi = self._expand(self._y.hi)
        return self._hi

    @property
    def lo(self):
        if self._lo is None:
            self._lo = self._expand(self._y.lo)
        return self._lo


def _mm(terms):
    la = jnp.concatenate([a.hi for a, _ in terms], axis=1)
    lb = jnp.concatenate([b.hi for _, b in terms], axis=0)
    return jnp.dot(la, lb, preferred_element_type=F32)


def _split3(x):
    hi = x.astype(BF16)
    r1 = x - hi.astype(F32)
    mid = r1.astype(BF16)
    lo = (r1 - mid.astype(F32)).astype(BF16)
    return [hi, mid, lo]


def _shift_rows(x, carry, n):
    row = lax.broadcasted_iota(jnp.int32, x.shape, 0)
    out = pltpu.roll(x, n, axis=0)
    for i in range(n):
        src = SUBLANES - n + i
        out = jnp.where(row == i, carry[src:src + 1, :], out)
    return out


def _sigmoid(x):
    return 1.0 / (1.0 + jnp.exp(-x))


def _interleave(order, stages):
    for name in order:
        next(stages[name], None)
    for gen in stages.values():
        for _ in gen:
            pass


_HALF_ORDER = ("advance", "inverse", "prep", "advance", "inverse", "advance", "inverse", "finish", "advance",
               "prep", "finish", "inverse", "inverse", "prep", "finish", "inverse", "inverse", "prep",
               "inverse", "inverse", "inverse", "inverse")


def _mixer_body(pm_ref, px_ref, mu_ref, vec_ref, wl_ref, ones_ref, tri_ref, o_ref,
                carry_rw, carry_u, state, s_at, s_rt, s_kh, s_bh, s_v, s_kd, s_bd, s_pe, s_y,
                r_x, r_tak, r_ark, r_arb, e_bonus, e_gate, e_yb):
    def vec(i):
        return vec_ref[i:i + 1, :]

    w0, a0, k_k, k_a, r_k, lnx_g, lnx_b, cw0, cw1, cw2 = (vec(i) for i in range(10))
    ps = range(N_PAIRS)

    def seg_sums(xs):
        tiles = [x[:, p * LANES:(p + 1) * LANES] for x in xs for p in ps]
        t = _Split(jnp.concatenate(tiles, axis=0))
        s = jnp.dot(jnp.concatenate([t.hi, t.lo], axis=1), ones_ref[...], preferred_element_type=F32)
        out = []
        for i in range(len(xs)):
            rows = [s[(i * N_PAIRS + p) * CHUNK:(i * N_PAIRS + p + 1) * CHUNK] for p in ps]
            out.append(jnp.concatenate(rows, axis=1))
        return out

    def prep(p_ref, row0, sel):
        p_rw = p_ref[0, row0:row0 + CHUNK, :RW_COLS]
        prev = _shift_rows(p_rw, carry_rw[...], 1)
        carry_rw[...] = p_rw[CHUNK - SUBLANES:, :]
        pm = p_rw + (prev - p_rw) * mu_ref[...]
        r = pm[:, 0 * D_RWKV:1 * D_RWKV]
        k = pm[:, 1 * D_RWKV:2 * D_RWKV]
        v = pm[:, 2 * D_RWKV:3 * D_RWKV]
        g_r = pm[:, 3 * D_RWKV:4 * D_RWKV]
        lora_in = pm[:, 4 * D_RWKV:]
        lane = lax.broadcasted_iota(jnp.int32, lora_in.shape, 1)
        lora_in = _Split(jnp.where(lane < D_LORA, jnp.tanh(lora_in), lora_in))
        yield
        lora = jnp.dot(jnp.concatenate([lora_in.hi, lora_in.hi, lora_in.lo], axis=1), wl_ref[...],
                       preferred_element_type=F32)
        logw = (-math.exp(-0.5)) * _sigmoid(w0 + lora[:, :D_RWKV])
        a = _sigmoid(a0 + lora[:, D_RWKV:])
        yield
        cum = jnp.dot(tri_ref[...], jnp.concatenate(_split3(logw), axis=0),
                      preferred_element_type=F32)
        kk = k * k_k
        k = k * (1.0 + (a - 1.0) * k_a)
        yield
        kk_sq, rk_sum = seg_sums([kk * kk, r * k * r_k])
        kk = kk * lax.rsqrt(jnp.maximum(kk_sq, 1e-24))
        beta = kk * a
        e_in = jnp.exp(cum)
        e_out = jnp.exp(-cum)
        e_ex = jnp.exp(cum - logw)
        p_end = e_in[CHUNK - 1:CHUNK, :]
        a_t = kk * e_ex
        r_t = r * e_in
        k_h = k * e_out
        b_h = beta * e_out
        k_d = k_h * p_end
        b_d = b_h * p_end
        for p in ps:
            sl = slice(p * LANES, (p + 1) * LANES)
            s_at[sel, p] = a_t[:, sl]
            s_rt[sel, p] = r_t[:, sl]
            s_kh[sel, p] = k_h[:, sl]
            s_bh[sel, p] = b_h[:, sl]
            s_v[sel, p] = v[:, sl]
            s_kd[sel, p] = k_d[:, sl]
            s_bd[sel, p] = b_d[:, sl]
            s_pe[sel, p] = jnp.broadcast_to(p_end[:, sl], (SUBLANES, LANES))
        e_bonus[sel] = rk_sum * v
        e_gate[sel] = g_r * _sigmoid(g_r)

        p_cv = p_ref[0, row0:row0 + CHUNK, RW_COLS:]
        b_g = p_cv[:, 0 * D_CONV:1 * D_CONV]
        c_g = p_cv[:, 1 * D_CONV:2 * D_CONV]
        h_c = p_cv[:, 2 * D_CONV:3 * D_CONV]
        g_c = p_cv[:, 3 * D_CONV:4 * D_CONV]
        u_c = c_g * h_c
        carry = carry_u[...]
        conv = cw0 * _shift_rows(u_c, carry, 2) + cw1 * _shift_rows(u_c, carry, 1) + cw2 * u_c
        carry_u[...] = u_c[CHUNK - SUBLANES:, :]
        e_yb[sel] = (b_g * conv * (g_c * _sigmoid(g_c))).astype(e_yb.dtype)

    t_i = lax.broadcasted_iota(jnp.int32, (CHUNK, LANES), 0)
    j_i = lax.broadcasted_iota(jnp.int32, (CHUNK, LANES), 1) & (HEAD - 1)

    def splits(xs):
        return [_Split(x) for x in xs]

    def pmm(xs, ys):
        return [_mm([(x, _BlockDiag(y))]) for x, y in zip(xs, ys)]

    def inverse(sel):
        strict = j_i < t_i
        incl = j_i <= t_i
        eye = (j_i == t_i).astype(F32)

        def same_block(log2_size):
            return (j_i >> log2_size) == (t_i >> log2_size)

        g = []
        for p in ps:
            lhs = jnp.concatenate([s_at[sel, p], s_rt[sel, p]], axis=0).astype(BF16)
            rhs_t = jnp.concatenate([_BlockDiag(_Split(s_bh[sel, p])).hi,
                                     _BlockDiag(_Split(s_kh[sel, p])).hi], axis=0)
            g.append(lax.dot_general(lhs, rhs_t, (((1,), (1,)), ((), ())), preferred_element_type=F32))
        t_ab = [jnp.where(strict, g_[:CHUNK, :LANES], 0.0) for g_ in g]
        for p in ps:
            r_arb[sel, p] = jnp.where(incl, g[p][CHUNK:, :LANES], 0.0)
            r_tak[sel, p] = jnp.where(strict, g[p][:CHUNK, LANES:], 0.0)
            r_ark[sel, p] = jnp.where(incl, g[p][CHUNK:, LANES:], 0.0)
        yield

        x = splits([eye - jnp.where(same_block(1), t, 0.0) for t in t_ab])
        for log2_size in range(2, 7):
            level = same_block(log2_size) & jnp.logical_not(same_block(log2_size - 1))
            c = splits([jnp.where(level, t, 0.0) for t in t_ab])
            xc = splits(pmm(x, c))
            yield
            x = splits([x_.x - m for x_, m in zip(x, pmm(xc, x))])
            if log2_size < 6:
                yield
        for p in ps:
            r_x[sel, p] = x[p].x

    def advance(sel):
        row = lax.broadcasted_iota(jnp.int32, (LANES, LANES), 0)
        col = lax.broadcasted_iota(jnp.int32, (LANES, LANES), 1)
        same_head = (row < HEAD) == (col < HEAD)
        vv = [s_v[sel, p] for p in ps]
        vbd = [_BlockDiag(_Split(v_)) for v_ in vv]
        s0 = [state[p] for p in ps]
        h0 = splits([s.T for s in s0])
        rhs = splits([_mm([(_Split(s_at[sel, p]), h0[p]), (_Split(r_tak[sel, p]), vbd[p])]) for p in ps])
        yield
        u = splits(pmm(splits([r_x[sel, p] for p in ps]), rhs))
        yield
        for p in ps:
            s_y[sel, p] = _mm([(_Split(s_rt[sel, p]), h0[p]), (_Split(r_ark[sel, p]), vbd[p]),
                               (_Split(-r_arb[sel, p]), _BlockDiag(u[p]))])
        yield
        for p in ps:
            wv_t = _Split(jnp.concatenate([vv[p], u[p].x], axis=0).T)
            upd = _mm([(wv_t, _Split(jnp.concatenate([s_kd[sel, p], -s_bd[sel, p]], axis=0)))])
            state[p] = jnp.where(same_head, s0[p] * s_pe[sel, p][0:1, :] + upd, 0.0)

    def finish(sel, row0):
        y = jnp.concatenate([s_y[sel, p] for p in ps], axis=1)
        inv_n = 1.0 / HEAD
        yield
        mean = seg_sums([y])[0] * inv_n
        yc = y - mean
        yield
        var = seg_sums([yc * yc])[0] * inv_n
        yn = yc * lax.rsqrt(var + LNX_EPS) * lnx_g + lnx_b
        y_a = (yn + e_bonus[sel]) * e_gate[sel]
        o_ref[0, row0:row0 + CHUNK, :D_RWKV] = y_a.astype(o_ref.dtype)
        o_ref[0, row0:row0 + CHUNK, D_RWKV:] = e_yb[sel]

    @pl.when(pl.program_id(1) == 0)
    def _():
        for ref in (carry_rw, carry_u, state, s_at, s_rt, s_v, s_kd, s_bd, s_pe,
                    r_x, r_tak, r_ark, r_arb, e_bonus, e_gate, e_yb):
            ref[...] = jnp.zeros_like(ref)
        for _ in prep(pm_ref, 0, 0):
            pass

    _interleave(_HALF_ORDER, {"advance": advance(1), "finish": finish(1, 0), "inverse": inverse(0),
                              "prep": prep(px_ref, 0, 1)})
    _interleave(_HALF_ORDER, {"advance": advance(0), "finish": finish(0, CHUNK), "inverse": inverse(1),
                              "prep": prep(px_ref, CHUNK, 0)})


def _mixer(proj, mu, vecs, w_lora3, seg_ones, tri3):
    bsz, rows, _ = proj.shape
    seq = rows - CHUNK
    assert seq % (2 * CHUNK) == 0
    n_steps = seq // (2 * CHUNK) + 1
    pair_buf = pltpu.VMEM((2, N_PAIRS, CHUNK, LANES), F32)
    row_buf = pltpu.VMEM((2, CHUNK, D_RWKV), F32)
    const = lambda shape: pl.BlockSpec(shape, lambda b, j: (0,) * len(shape))
    return pl.pallas_call(
        _mixer_body,
        grid=(bsz, n_steps),
        in_specs=[
            pl.BlockSpec((1, CHUNK, IN_COLS), lambda b, j: (b, seq // CHUNK, 0)),
            pl.BlockSpec((1, 2 * CHUNK, IN_COLS), lambda b, j: (b, jnp.minimum(j, n_steps - 2), 0)),
            const(mu.shape), const(vecs.shape), const(w_lora3.shape),
            const(seg_ones.shape), const(tri3.shape),
        ],
        out_specs=pl.BlockSpec((1, 2 * CHUNK, D_MODEL), lambda b, j: (b, jnp.maximum(j - 1, 0), 0)),
        out_shape=jax.ShapeDtypeStruct((bsz, seq, D_MODEL), BF16),
        scratch_shapes=[
            pltpu.VMEM((SUBLANES, RW_COLS), F32),
            pltpu.VMEM((SUBLANES, D_CONV), F32),
            pltpu.VMEM((N_PAIRS, LANES, LANES), F32),
            pair_buf, pair_buf, pair_buf, pair_buf, pair_buf, pair_buf, pair_buf,
            pltpu.VMEM((2, N_PAIRS, SUBLANES, LANES), F32),
            pair_buf,
            pair_buf, pair_buf, pair_buf, pair_buf,
            row_buf, row_buf,
            pltpu.VMEM((2, CHUNK, D_CONV), BF16),
        ],
        compiler_params=pltpu.CompilerParams(
            dimension_semantics=("arbitrary", "arbitrary"),
            vmem_limit_bytes=VMEM_LIMIT),
        name="mixer",
    )(proj, proj, mu, vecs, w_lora3, seg_ones, tri3)


def _outproj_body(mix_ref, x_ref, w_ref, g_ref, o_ref):
    h = x_ref[...] + jnp.dot(mix_ref[...], w_ref[...], preferred_element_type=F32)
    ms = jnp.mean(h * h, axis=-1, keepdims=True)
    o_ref[...] = h * lax.rsqrt(ms + RMS_EPS) * g_ref[...]


def _outproj(mix2d, x2d, w_bf16, g):
    rows = x2d.shape[0]
    tm = OUTPROJ_TM
    assert rows % tm == 0
    return pl.pallas_call(
        _outproj_body,
        grid=(rows // tm,),
        in_specs=[
            pl.BlockSpec((tm, D_MODEL), lambda i: (i, 0)),
            pl.BlockSpec((tm, D_MODEL), lambda i: (i, 0)),
            pl.BlockSpec((D_MODEL, D_MODEL), lambda i: (0, 0)),
            pl.BlockSpec((1, D_MODEL), lambda i: (0, 0)),
        ],
        out_specs=pl.BlockSpec((tm, D_MODEL), lambda i: (i, 0)),
        out_shape=jax.ShapeDtypeStruct((rows, D_MODEL), F32),
        compiler_params=pltpu.CompilerParams(
            dimension_semantics=("arbitrary",),
            vmem_limit_bytes=VMEM_LIMIT),
        name="outproj",
    )(mix2d, x2d, w_bf16, g)


def _stack_split(w):
    hi = w.astype(BF16)
    lo = (w - hi.astype(F32)).astype(BF16)
    return jnp.concatenate([hi, lo, hi], axis=0)


def kernel(x, meta_tokens, norm_in_g, w_in, mu_shift, w0, w_lora_up, a0, a_lora_up, k_k, k_a, r_k, lnx_g, lnx_b, conv_w, w_out, norm_f_g):
    bsz, seq, _ = x.shape
    assert norm_in_g.shape[0] == 1 and seq % CHUNK == 0
    x2d = x.reshape(bsz * seq, D_MODEL)

    front = jnp.concatenate([jnp.zeros((FRONT_PAD, D_MODEL), x.dtype), meta_tokens.astype(x.dtype)], axis=0)
    g_in = norm_in_g[0][None, :]
    w_in_bf16 = w_in[0].astype(BF16)
    proj = _inproj_meta(front, g_in, w_in_bf16, _inproj(x, g_in, w_in_bf16))

    w_lora = jnp.zeros((2 * D_LORA, 2 * D_RWKV), F32)
    w_lora = w_lora.at[:D_LORA, :D_RWKV].set(w_lora_up[0]).at[D_LORA:, D_RWKV:].set(a_lora_up[0])
    vecs = jnp.concatenate([
        w0[0][None], a0[0][None], k_k[0][None], k_a[0][None], r_k[0].reshape(1, D_RWKV),
        lnx_g[0][None], lnx_b[0][None], conv_w[0],
        jnp.zeros((16 - 10, D_RWKV), F32)], axis=0)
    lane_head = jnp.arange(LANES) // HEAD
    ones_bd = (lane_head[:, None] == lane_head[None, :]).astype(BF16)
    seg_ones = jnp.concatenate([ones_bd, ones_bd], axis=0)
    t_idx = jnp.arange(CHUNK)
    tri = (t_idx[None, :] <= t_idx[:, None]).astype(BF16)
    tri3 = jnp.concatenate([tri, tri, tri], axis=1)

    mix = _mixer(proj, mu_shift, vecs, _stack_split(w_lora), seg_ones, tri3)

    out = _outproj(mix.reshape(bsz * seq, D_MODEL), x2d, w_out[0].astype(BF16), norm_f_g[None, :])
    return out.reshape(bsz, seq, D_MODEL)
```

```python
import math

import jax
import jax.numpy as jnp
from jax import lax
from jax.experimental import pallas as pl
from jax.experimental.pallas import tpu as pltpu

F32 = jnp.float32
BF16 = jnp.bfloat16

D_MODEL = 2048
N_META = 16
D_RWKV = 1024
HEAD = 64
N_HEADS = D_RWKV // HEAD
D_LORA = 64
D_CONV = 1024
RW_COLS = 4 * D_RWKV + 2 * D_LORA
IN_COLS = RW_COLS + 4 * D_CONV
RMS_EPS = 1e-6
LNX_EPS = 64e-5

CHUNK = 64
LANES = 128
SUBLANES = 8
N_PAIRS = D_RWKV // LANES
FRONT_PAD = CHUNK - N_META

VMEM_LIMIT = 56 * 1024 * 1024

INPROJ_TM = 1024
INPROJ_TN = 1664
OUTPROJ_TM = 512


def _rmsnorm_bf16(x, g):
    ms = jnp.mean(x * x, axis=-1, keepdims=True)
    return (x * lax.rsqrt(ms + RMS_EPS) * g).astype(BF16)


def _inproj_body(x_ref, g_ref, w_ref, o_ref, hn_ref):
    @pl.when(pl.program_id(2) == 0)
    def _():
        hn_ref[...] = _rmsnorm_bf16(x_ref[0], g_ref[...])

    o_ref[0] = jnp.dot(hn_ref[...], w_ref[...], preferred_element_type=F32)


def _inproj(x, g, w_bf16):
    bsz, seq, _ = x.shape
    tm, tn = INPROJ_TM, INPROJ_TN
    assert seq % tm == 0 and IN_COLS % tn == 0
    return pl.pallas_call(
        _inproj_body,
        grid=(bsz, seq // tm, IN_COLS // tn),
        in_specs=[
            pl.BlockSpec((1, tm, D_MODEL), lambda b, i, j: (b, i, 0)),
            pl.BlockSpec((1, D_MODEL), lambda b, i, j: (0, 0)),
            pl.BlockSpec((D_MODEL, tn), lambda b, i, j: (0, j)),
        ],
        out_specs=pl.BlockSpec((1, tm, tn), lambda b, i, j: (b, i, j)),
        out_shape=jax.ShapeDtypeStruct((bsz, seq, IN_COLS), F32),
        scratch_shapes=[pltpu.VMEM((tm, D_MODEL), BF16)],
        compiler_params=pltpu.CompilerParams(
            dimension_semantics=("arbitrary", "arbitrary", "arbitrary"),
            vmem_limit_bytes=VMEM_LIMIT),
        name="inproj",
    )(x, g, w_bf16)


def _inproj_meta_body(x_ref, g_ref, w_ref, o_ref):
    o_ref[0] = jnp.dot(_rmsnorm_bf16(x_ref[...], g_ref[...]), w_ref[...], preferred_element_type=F32)


def _inproj_meta(front, g, w_bf16):
    tn = INPROJ_TN
    return pl.pallas_call(
        _inproj_meta_body,
        grid=(IN_COLS // tn,),
        in_specs=[
            pl.BlockSpec((CHUNK, D_MODEL), lambda j: (0, 0)),
            pl.BlockSpec((1, D_MODEL), lambda j: (0, 0)),
            pl.BlockSpec((D_MODEL, tn), lambda j: (0, j)),
        ],
        out_specs=pl.BlockSpec((1, CHUNK, tn), lambda j: (0, 0, j)),
        out_shape=jax.ShapeDtypeStruct((1, CHUNK, IN_COLS), F32),
        compiler_params=pltpu.CompilerParams(
            dimension_semantics=("arbitrary",),
            vmem_limit_bytes=VMEM_LIMIT),
        name="inproj_meta",
    )(front, g, w_bf16)


class _Split:
    def __init__(self, x):
        self.x = x
        self._hi = None
        self._lo = None

    @property
    def hi(self):
        if self._hi is None:
            self._hi = self.x.astype(BF16)
        return self._hi

    @property
    def lo(self):
        if self._lo is None:
            self._lo = (self.x - self.hi.astype(F32)).astype(BF16)
        return self._lo


class _BlockDiag:
    def __init__(self, y):
        self._y = y
        self._hi = None

    @property
    def hi(self):
        if self._hi is None:
            yp = self._y.hi
            lane = lax.broadcasted_iota(jnp.int32, yp.shape, 1)
            first = lane < HEAD
            zero = jnp.zeros_like(yp)
            self._hi = jnp.concatenate([jnp.where(first, yp, zero), jnp.where(first, zero, yp)], axis=0)
        return self._hi


def _mm(terms):
    la = jnp.concatenate([a.hi for a, _ in terms], axis=1)
    lb = jnp.concatenate([b.hi for _, b in terms], axis=0)
    return jnp.dot(la, lb, preferred_element_type=F32)


def _shift_rows(x, carry, n):
    row = lax.broadcasted_iota(jnp.int32, x.shape, 0)
    out = pltpu.roll(x, n, axis=0)
    for i in range(n):
        src = SUBLANES - n + i
        out = jnp.where(row == i, carry[src:src + 1, :], out)
    return out


def _sigmoid(x):
    return 0.5 + 0.5 * jnp.tanh(0.5 * x)


def _silu(x):
    h = 0.5 * x
    return h + h * jnp.tanh(h)


def _interleave(order, stages):
    for name in order:
        next(stages[name], None)
    for gen in stages.values():
        for _ in gen:
            pass


_HALF_ORDER = ("advance", "inverse", "prep", "advance", "inverse", "advance", "inverse", "finish", "advance",
               "prep", "finish", "inverse", "inverse", "prep", "finish", "inverse", "inverse", "prep",
               "inverse", "inverse", "inverse", "inverse")


def _mixer_body(pm_ref, px_ref, mu_ref, vec_ref, wl_ref, ones_ref, tri_ref, o_ref,
                carry_rw, carry_u, state, s_at, s_rt, s_kh, s_bh, s_v, s_kd, s_bd, s_pe, s_y,
                r_x, r_tak, r_ark, r_arb, e_bonus, e_gate, e_yb):
    def vec(i):
        return vec_ref[i:i + 1, :]

    w0, a0, k_k, k_a, r_k, lnx_g, lnx_b, cw0, cw1, cw2 = (vec(i) for i in range(10))
    ps = range(N_PAIRS)

    def seg_sums(xs):
        tiles = [x[:, p * LANES:(p + 1) * LANES] for x in xs for p in ps]
        t = jnp.concatenate(tiles, axis=0).astype(BF16)
        s = jnp.dot(t, ones_ref[...], preferred_element_type=F32)
        out = []
        for i in range(len(xs)):
            rows = [s[(i * N_PAIRS + p) * CHUNK:(i * N_PAIRS + p + 1) * CHUNK] for p in ps]
            out.append(jnp.concatenate(rows, axis=1))
        return out

    def prep(p_ref, row0, sel):
        p_rw = p_ref[0, row0:row0 + CHUNK, :RW_COLS]
        prev = _shift_rows(p_rw, carry_rw[...], 1)
        carry_rw[...] = p_rw[CHUNK - SUBLANES:, :]
        pm = p_rw + (prev - p_rw) * mu_ref[...]
        r = pm[:, 0 * D_RWKV:1 * D_RWKV]
        k = pm[:, 1 * D_RWKV:2 * D_RWKV]
        v = pm[:, 2 * D_RWKV:3 * D_RWKV]
        g_r = pm[:, 3 * D_RWKV:4 * D_RWKV]
        lora_in = pm[:, 4 * D_RWKV:]
        lane = lax.broadcasted_iota(jnp.int32, lora_in.shape, 1)
        lora_in = jnp.where(lane < D_LORA, jnp.tanh(lora_in), lora_in).astype(BF16)
        yield
        lora = jnp.dot(lora_in, wl_ref[...], preferred_element_type=F32)
        logw = (-math.exp(-0.5)) * _sigmoid(w0 + lora[:, :D_RWKV])
        a = _sigmoid(a0 + lora[:, D_RWKV:])
        yield
        logw = _Split(logw)
        cum = jnp.dot(tri_ref[...], jnp.concatenate([logw.hi, logw.lo], axis=0),
                      preferred_element_type=F32)
        logw = logw.x
        kk = k * k_k
        k = k * (1.0 + (a - 1.0) * k_a)
        yield
        kk_sq, rk_sum = seg_sums([kk * kk, r * k * r_k])
        kk = kk * lax.rsqrt(jnp.maximum(kk_sq, 1e-24))
        beta = kk * a
        e_in = jnp.exp(cum)
        e_out = jnp.exp(-cum)
        e_ex = jnp.exp(cum - logw)
        p_end = e_in[CHUNK - 1:CHUNK, :]
        a_t = kk * e_ex
        r_t = r * e_in
        k_h = k * e_out
        b_h = beta * e_out
        k_d = k_h * p_end
        b_d = b_h * p_end
        for dst, val in ((s_at, a_t), (s_rt, r_t), (s_kh, k_h), (s_bh, b_h), (s_v, v), (s_kd, k_d), (s_bd, b_d)):
            val = val.astype(dst.dtype)
            for p in ps:
                dst[sel, p] = val[:, p * LANES:(p + 1) * LANES]
        for p in ps:
            s_pe[sel, p] = jnp.broadcast_to(p_end[:, p * LANES:(p + 1) * LANES], (SUBLANES, LANES))
        e_bonus[sel] = rk_sum * v
        e_gate[sel] = _silu(g_r)

        p_cv = p_ref[0, row0:row0 + CHUNK, RW_COLS:]
        b_g = p_cv[:, 0 * D_CONV:1 * D_CONV]
        c_g = p_cv[:, 1 * D_CONV:2 * D_CONV]
        h_c = p_cv[:, 2 * D_CONV:3 * D_CONV]
        g_c = p_cv[:, 3 * D_CONV:4 * D_CONV]
        u_c = c_g * h_c
        carry = carry_u[...]
        conv = cw0 * _shift_rows(u_c, carry, 2) + cw1 * _shift_rows(u_c, carry, 1) + cw2 * u_c
        carry_u[...] = u_c[CHUNK - SUBLANES:, :]
        e_yb[sel] = (b_g * conv * _silu(g_c)).astype(e_yb.dtype)

    t_i = lax.broadcasted_iota(jnp.int32, (CHUNK, LANES), 0)
    j_i = lax.broadcasted_iota(jnp.int32, (CHUNK, LANES), 1) & (HEAD - 1)

    def splits(xs):
        return [_Split(x) for x in xs]

    def pmm(xs, ys):
        return [_mm([(x, _BlockDiag(y))]) for x, y in zip(xs, ys)]

    def inverse(sel):
        strict = j_i < t_i
        incl = j_i <= t_i
        eye = (j_i == t_i).astype(F32)

        def same_block(log2_size):
            return (j_i >> log2_size) == (t_i >> log2_size)

        g = []
        for p in ps:
            lhs = jnp.concatenate([s_at[sel, p], s_rt[sel, p]], axis=0)
            rhs_t = jnp.concatenate([_BlockDiag(_Split(s_bh[sel, p])).hi,
                                     _BlockDiag(_Split(s_kh[sel, p])).hi], axis=0)
            g.append(lax.dot_general(lhs, rhs_t, (((1,), (1,)), ((), ())), preferred_element_type=F32))
        t_ab = [jnp.where(strict, g_[:CHUNK, :LANES], 0.0) for g_ in g]
        for p in ps:
            r_arb[sel, p] = jnp.where(incl, g[p][CHUNK:, :LANES], 0.0).astype(r_arb.dtype)
            r_tak[sel, p] = jnp.where(strict, g[p][:CHUNK, LANES:], 0.0).astype(r_tak.dtype)
            r_ark[sel, p] = jnp.where(incl, g[p][CHUNK:, LANES:], 0.0).astype(r_ark.dtype)
        yield

        x = splits([eye - jnp.where(same_block(1), t, 0.0) for t in t_ab])
        for log2_size in range(2, 7):
            level = same_block(log2_size) & jnp.logical_not(same_block(log2_size - 1))
            c = splits([jnp.where(level, t, 0.0) for t in t_ab])
            xc = splits(pmm(x, c))
            yield
            x = splits([x_.x - m for x_, m in zip(x, pmm(xc, x))])
            if log2_size < 6:
                yield
        for p in ps:
            r_x[sel, p] = x[p].hi

    def advance(sel):
        row = lax.broadcasted_iota(jnp.int32, (LANES, LANES), 0)
        col = lax.broadcasted_iota(jnp.int32, (LANES, LANES), 1)
        same_head = (row < HEAD) == (col < HEAD)
        vv = [s_v[sel, p] for p in ps]
        vbd = [_BlockDiag(_Split(v_)) for v_ in vv]
        s0 = [state[p] for p in ps]
        h0 = splits([s.T for s in s0])
        rhs = splits([_mm([(_Split(s_at[sel, p]), h0[p]), (_Split(r_tak[sel, p]), vbd[p])]) for p in ps])
        yield
        u = splits(pmm(splits([r_x[sel, p] for p in ps]), rhs))
        yield
        for p in ps:
            s_y[sel, p] = _mm([(_Split(s_rt[sel, p]), h0[p]), (_Split(r_ark[sel, p]), vbd[p]),
                               (_Split(-r_arb[sel, p]), _BlockDiag(u[p]))])
        yield
        for p in ps:
            wv_t = _Split(jnp.concatenate([vv[p], u[p].x], axis=0).T)
            upd = _mm([(wv_t, _Split(jnp.concatenate([s_kd[sel, p], -s_bd[sel, p]], axis=0)))])
            state[p] = jnp.where(same_head, s0[p] * s_pe[sel, p][0:1, :] + upd, 0.0)

    def finish(sel, row0):
        y = jnp.concatenate([s_y[sel, p] for p in ps], axis=1)
        inv_n = 1.0 / HEAD
        yield
        mean = seg_sums([y])[0] * inv_n
        yc = y - mean
        yield
        var = seg_sums([yc * yc])[0] * inv_n
        yn = yc * lax.rsqrt(var + LNX_EPS) * lnx_g + lnx_b
        y_a = (yn + e_bonus[sel]) * e_gate[sel]
        o_ref[0, row0:row0 + CHUNK, :D_RWKV] = y_a.astype(o_ref.dtype)
        o_ref[0, row0:row0 + CHUNK, D_RWKV:] = e_yb[sel]

    @pl.when(pl.program_id(1) == 0)
    def _():
        for ref in (carry_rw, carry_u, state, s_at, s_rt, s_v, s_kd, s_bd, s_pe,
                    r_x, r_tak, r_ark, r_arb, e_bonus, e_gate, e_yb):
            ref[...] = jnp.zeros_like(ref)
        for _ in prep(pm_ref, 0, 0):
            pass

    _interleave(_HALF_ORDER, {"advance": advance(1), "finish": finish(1, 0), "inverse": inverse(0),
                              "prep": prep(px_ref, 0, 1)})
    _interleave(_HALF_ORDER, {"advance": advance(0), "finish": finish(0, CHUNK), "inverse": inverse(1),
                              "prep": prep(px_ref, CHUNK, 0)})


def _mixer(proj_meta, proj, mu, vecs, w_lora, seg_ones, tri2):
    bsz, seq, _ = proj.shape
    assert seq % (2 * CHUNK) == 0
    n_steps = seq // (2 * CHUNK) + 1
    pair_buf = pltpu.VMEM((2, N_PAIRS, CHUNK, LANES), F32)
    pair_bf16 = pltpu.VMEM((2, N_PAIRS, CHUNK, LANES), BF16)
    row_buf = pltpu.VMEM((2, CHUNK, D_RWKV), F32)
    const = lambda shape: pl.BlockSpec(shape, lambda b, j: (0,) * len(shape))
    return pl.pallas_call(
        _mixer_body,
        grid=(bsz, n_steps),
        in_specs=[
            const(proj_meta.shape),
            pl.BlockSpec((1, 2 * CHUNK, IN_COLS), lambda b, j: (b, jnp.minimum(j, n_steps - 2), 0)),
            const(mu.shape), const(vecs.shape), const(w_lora.shape),
            const(seg_ones.shape), const(tri2.shape),
        ],
        out_specs=pl.BlockSpec((1, 2 * CHUNK, D_MODEL), lambda b, j: (b, jnp.maximum(j - 1, 0), 0)),
        out_shape=jax.ShapeDtypeStruct((bsz, seq, D_MODEL), BF16),
        scratch_shapes=[
            pltpu.VMEM((SUBLANES, RW_COLS), F32),
            pltpu.VMEM((SUBLANES, D_CONV), F32),
            pltpu.VMEM((N_PAIRS, LANES, LANES), F32),
            pair_bf16, pair_bf16, pair_bf16, pair_bf16, pair_buf, pair_bf16, pair_bf16,
            pltpu.VMEM((2, N_PAIRS, SUBLANES, LANES), F32),
            pair_buf,
            pair_bf16, pair_bf16, pair_bf16, pair_bf16,
            row_buf, row_buf,
            pltpu.VMEM((2, CHUNK, D_CONV), BF16),
        ],
        compiler_params=pltpu.CompilerParams(
            dimension_semantics=("arbitrary", "arbitrary"),
            vmem_limit_bytes=VMEM_LIMIT),
        name="mixer",
    )(proj_meta, proj, mu, vecs, w_lora, seg_ones, tri2)


def _outproj_body(mix_ref, x_ref, w_ref, g_ref, o_ref):
    h = x_ref[...] + jnp.dot(mix_ref[...], w_ref[...], preferred_element_type=F32)
    ms = jnp.mean(h * h, axis=-1, keepdims=True)
    o_ref[...] = h * lax.rsqrt(ms + RMS_EPS) * g_ref[...]


def _outproj(mix2d, x2d, w_bf16, g):
    rows = x2d.shape[0]
    tm = OUTPROJ_TM
    assert rows % tm == 0
    return pl.pallas_call(
        _outproj_body,
        grid=(rows // tm,),
        in_specs=[
            pl.BlockSpec((tm, D_MODEL), lambda i: (i, 0)),
            pl.BlockSpec((tm, D_MODEL), lambda i: (i, 0)),
            pl.BlockSpec((D_MODEL, D_MODEL), lambda i: (0, 0)),
            pl.BlockSpec((1, D_MODEL), lambda i: (0, 0)),
        ],
        out_specs=pl.BlockSpec((tm, D_MODEL), lambda i: (i, 0)),
        out_shape=jax.ShapeDtypeStruct((rows, D_MODEL), F32),
        compiler_params=pltpu.CompilerParams(
            dimension_semantics=("arbitrary",),
            vmem_limit_bytes=VMEM_LIMIT),
        name="outproj",
    )(mix2d, x2d, w_bf16, g)


def kernel(x, meta_tokens, norm_in_g, w_in, mu_shift, w0, w_lora_up, a0, a_lora_up, k_k, k_a, r_k, lnx_g, lnx_b, conv_w, w_out, norm_f_g):
    bsz, seq, _ = x.shape
    assert norm_in_g.shape[0] == 1 and seq % CHUNK == 0
    x2d = x.reshape(bsz * seq, D_MODEL)

    front = jnp.concatenate([jnp.zeros((FRONT_PAD, D_MODEL), x.dtype), meta_tokens.astype(x.dtype)], axis=0)
    g_in = norm_in_g[0][None, :]
    w_in_bf16 = w_in[0].astype(BF16)
    proj_meta = _inproj_meta(front, g_in, w_in_bf16)
    proj = _inproj(x, g_in, w_in_bf16)

    w_lora = jnp.zeros((2 * D_LORA, 2 * D_RWKV), F32)
    w_lora = w_lora.at[:D_LORA, :D_RWKV].set(w_lora_up[0]).at[D_LORA:, D_RWKV:].set(a_lora_up[0])
    vecs = jnp.concatenate([
        w0[0][None], a0[0][None], k_k[0][None], k_a[0][None], r_k[0].reshape(1, D_RWKV),
        lnx_g[0][None], lnx_b[0][None], conv_w[0],
        jnp.zeros((16 - 10, D_RWKV), F32)], axis=0)
    lane_head = jnp.arange(LANES) // HEAD
    seg_ones = (lane_head[:, None] == lane_head[None, :]).astype(BF16)
    t_idx = jnp.arange(CHUNK)
    tri = (t_idx[None, :] <= t_idx[:, None]).astype(BF16)
    tri2 = jnp.concatenate([tri, tri], axis=1)

    mix = _mixer(proj_meta, proj, mu_shift, vecs, w_lora.astype(BF16), seg_ones, tri2)

    out = _outproj(mix.reshape(bsz * seq, D_MODEL), x2d, w_out[0].astype(BF16), norm_f_g[None, :])
    return out.reshape(bsz, seq, D_MODEL)
```

```python
import math

import jax
import jax.numpy as jnp
from jax import lax
from jax.experimental import pallas as pl
from jax.experimental.pallas import tpu as pltpu

F32 = jnp.float32
BF16 = jnp.bfloat16

D_MODEL = 2048
N_META = 16
D_RWKV = 1024
HEAD = 64
N_HEADS = D_RWKV // HEAD
D_LORA = 64
D_CONV = 1024
RW_COLS = 4 * D_RWKV + 2 * D_LORA
IN_COLS = RW_COLS + 4 * D_CONV
LORA_COLS = 2 * D_LORA
MAIN_COLS = IN_COLS - LORA_COLS
RMS_EPS = 1e-6
LNX_EPS = 64e-5

CHUNK = 64
LANES = 128
SUBLANES = 8
N_PAIRS = D_RWKV // LANES
FRONT_PAD = CHUNK - N_META

VMEM_LIMIT = 56 * 1024 * 1024
INPROJ_VMEM_LIMIT = 60 * 1024 * 1024

INPROJ_TM = 1024
INPROJ_TN = 2048
OUTPROJ_TM = 512


def _rmsnorm_bf16(x, g):
    ms = jnp.mean(x * x, axis=-1, keepdims=True)
    return (x * lax.rsqrt(ms + RMS_EPS) * g).astype(BF16)


def _inproj_body(x_ref, g_ref, w_ref, wl_ref, o_ref, ol_ref, hn_ref):
    @pl.when(pl.program_id(2) == 0)
    def _():
        hn_ref[...] = _rmsnorm_bf16(x_ref[0], g_ref[...])
        ol_ref[0] = jnp.dot(hn_ref[...], wl_ref[...], preferred_element_type=F32)

    o_ref[0] = jnp.dot(hn_ref[...], w_ref[...], preferred_element_type=F32)


def _inproj(x, g, w_main, w_lora_in):
    bsz, seq, _ = x.shape
    tm, tn = INPROJ_TM, INPROJ_TN
    assert seq % tm == 0 and MAIN_COLS % tn == 0
    return pl.pallas_call(
        _inproj_body,
        grid=(bsz, seq // tm, MAIN_COLS // tn),
        in_specs=[
            pl.BlockSpec((1, tm, D_MODEL), lambda b, i, j: (b, i, 0)),
            pl.BlockSpec((1, D_MODEL), lambda b, i, j: (0, 0)),
            pl.BlockSpec((D_MODEL, tn), lambda b, i, j: (0, j)),
            pl.BlockSpec((D_MODEL, LORA_COLS), lambda b, i, j: (0, 0)),
        ],
        out_specs=[pl.BlockSpec((1, tm, tn), lambda b, i, j: (b, i, j)),
                   pl.BlockSpec((1, tm, LORA_COLS), lambda b, i, j: (b, i, 0))],
        out_shape=[jax.ShapeDtypeStruct((bsz, seq, MAIN_COLS), F32),
                   jax.ShapeDtypeStruct((bsz, seq, LORA_COLS), F32)],
        scratch_shapes=[pltpu.VMEM((tm, D_MODEL), BF16)],
        compiler_params=pltpu.CompilerParams(
            dimension_semantics=("arbitrary", "arbitrary", "arbitrary"),
            vmem_limit_bytes=INPROJ_VMEM_LIMIT),
        name="inproj",
    )(x, g, w_main, w_lora_in)


def _inproj_meta_body(x_ref, g_ref, w_ref, wl_ref, o_ref, ol_ref):
    hn = _rmsnorm_bf16(x_ref[...], g_ref[...])

    @pl.when(pl.program_id(0) == 0)
    def _():
        ol_ref[0] = jnp.dot(hn, wl_ref[...], preferred_element_type=F32)

    o_ref[0] = jnp.dot(hn, w_ref[...], preferred_element_type=F32)


def _inproj_meta(front, g, w_main, w_lora_in):
    tn = INPROJ_TN
    return pl.pallas_call(
        _inproj_meta_body,
        grid=(MAIN_COLS // tn,),
        in_specs=[
            pl.BlockSpec((CHUNK, D_MODEL), lambda j: (0, 0)),
            pl.BlockSpec((1, D_MODEL), lambda j: (0, 0)),
            pl.BlockSpec((D_MODEL, tn), lambda j: (0, j)),
            pl.BlockSpec((D_MODEL, LORA_COLS), lambda j: (0, 0)),
        ],
        out_specs=[pl.BlockSpec((1, CHUNK, tn), lambda j: (0, 0, j)),
                   pl.BlockSpec((1, CHUNK, LORA_COLS), lambda j: (0, 0, 0))],
        out_shape=[jax.ShapeDtypeStruct((1, CHUNK, MAIN_COLS), F32),
                   jax.ShapeDtypeStruct((1, CHUNK, LORA_COLS), F32)],
        compiler_params=pltpu.CompilerParams(
            dimension_semantics=("arbitrary",),
            vmem_limit_bytes=VMEM_LIMIT),
        name="inproj_meta",
    )(front, g, w_main, w_lora_in)


class _Split:
    def __init__(self, x):
        self.x = x
        self._hi = None
        self._lo = None

    @property
    def hi(self):
        if self._hi is None:
            self._hi = self.x.astype(BF16)
        return self._hi

    @property
    def lo(self):
        if self._lo is None:
            self._lo = (self.x - self.hi.astype(F32)).astype(BF16)
        return self._lo


class _BlockDiag:
    def __init__(self, y):
        self._y = y
        self._hi = None

    @property
    def hi(self):
        if self._hi is None:
            yp = self._y.hi
            lane = lax.broadcasted_iota(jnp.int32, yp.shape, 1)
            first = lane < HEAD
            zero = jnp.zeros_like(yp)
            self._hi = jnp.concatenate([jnp.where(first, yp, zero), jnp.where(first, zero, yp)], axis=0)
        return self._hi


def _mm(terms):
    la = jnp.concatenate([a.hi for a, _ in terms], axis=1)
    lb = jnp.concatenate([b.hi for _, b in terms], axis=0)
    return jnp.dot(la, lb, preferred_element_type=F32)


def _shift_rows(x, carry, n):
    row = lax.broadcasted_iota(jnp.int32, x.shape, 0)
    out = pltpu.roll(x, n, axis=0)
    for i in range(n):
        src = SUBLANES - n + i
        out = jnp.where(row == i, carry[src:src + 1, :], out)
    return out


def _sigmoid(x):
    return 0.5 + 0.5 * jnp.tanh(0.5 * x)


def _silu(x):
    h = 0.5 * x
    return h + h * jnp.tanh(h)


def _interleave(order, stages):
    for name in order:
        next(stages[name], None)
    for gen in stages.values():
        for _ in gen:
            pass


_HALF_ORDER = ("advance", "inverse", "prep", "advance", "inverse", "advance", "inverse", "finish", "advance",
               "prep", "finish", "inverse", "inverse", "prep", "finish", "inverse", "inverse", "prep",
               "inverse", "inverse", "inverse", "inverse")


def _mixer_body(pm_ref, lm_ref, px_ref, lx_ref, mu_ref, vec_ref, wl_ref, ones_ref, tri_ref, o_ref,
                carry_rw, carry_u, state, s_at, s_rt, s_kh, s_bh, s_v, s_kd, s_bd, s_pe, s_y,
                r_x, r_tak, r_ark, r_arb, e_bonus, e_gate, e_yb):
    def vec(i):
        return vec_ref[i:i + 1, :]

    w0, a0, k_k, k_a, r_k, lnx_g, lnx_b, cw0, cw1, cw2 = (vec(i) for i in range(10))
    ps = range(N_PAIRS)

    def seg_sums(xs):
        tiles = [x[:, p * LANES:(p + 1) * LANES] for x in xs for p in ps]
        t = jnp.concatenate(tiles, axis=0).astype(BF16)
        s = jnp.dot(t, ones_ref[...], preferred_element_type=F32)
        out = []
        for i in range(len(xs)):
            rows = [s[(i * N_PAIRS + p) * CHUNK:(i * N_PAIRS + p + 1) * CHUNK] for p in ps]
            out.append(jnp.concatenate(rows, axis=1))
        return out

    def prep(p_ref, l_ref, row0, sel):
        p_rw = jnp.concatenate([p_ref[0, row0:row0 + CHUNK, :4 * D_RWKV], l_ref[0, row0:row0 + CHUNK, :]],
                               axis=1)
        prev = _shift_rows(p_rw, carry_rw[...], 1)
        carry_rw[...] = p_rw[CHUNK - SUBLANES:, :]
        pm = p_rw + (prev - p_rw) * mu_ref[...]
        r = pm[:, 0 * D_RWKV:1 * D_RWKV]
        k = pm[:, 1 * D_RWKV:2 * D_RWKV]
        v = pm[:, 2 * D_RWKV:3 * D_RWKV]
        g_r = pm[:, 3 * D_RWKV:4 * D_RWKV]
        lora_in = pm[:, 4 * D_RWKV:]
        lane = lax.broadcasted_iota(jnp.int32, lora_in.shape, 1)
        lora_in = jnp.where(lane < D_LORA, jnp.tanh(lora_in), lora_in).astype(BF16)
        yield
        lora = jnp.dot(lora_in, wl_ref[...], preferred_element_type=F32)
        logw = (-math.exp(-0.5)) * _sigmoid(w0 + lora[:, :D_RWKV])
        a = _sigmoid(a0 + lora[:, D_RWKV:])
        yield
        logw = _Split(logw)
        cum = jnp.dot(tri_ref[...], jnp.concatenate([logw.hi, logw.lo], axis=0),
                      preferred_element_type=F32)
        logw = logw.x
        kk = k * k_k
        k = k * (1.0 + (a - 1.0) * k_a)
        yield
        kk_sq, rk_sum = seg_sums([kk * kk, r * k * r_k])
        kk = kk * lax.rsqrt(jnp.maximum(kk_sq, 1e-24))
        beta = kk * a
        e_in = jnp.exp(cum)
        e_out = jnp.exp(-cum)
        e_ex = jnp.exp(cum - logw)
        p_end = e_in[CHUNK - 1:CHUNK, :]
        a_t = kk * e_ex
        r_t = r * e_in
        k_h = k * e_out
        b_h = beta * e_out
        k_d = k_h * p_end
        b_d = b_h * p_end
        for dst, val in ((s_at, a_t), (s_rt, r_t), (s_kh, k_h), (s_bh, b_h), (s_v, v), (s_kd, k_d), (s_bd, b_d)):
            val = val.astype(dst.dtype)
            for p in ps:
                dst[sel, p] = val[:, p * LANES:(p + 1) * LANES]
        for p in ps:
            s_pe[sel, p] = jnp.broadcast_to(p_end[:, p * LANES:(p + 1) * LANES], (SUBLANES, LANES))
        e_bonus[sel] = rk_sum * v
        e_gate[sel] = _silu(g_r)

        p_cv = p_ref[0, row0:row0 + CHUNK, 4 * D_RWKV:]
        b_g = p_cv[:, 0 * D_CONV:1 * D_CONV]
        c_g = p_cv[:, 1 * D_CONV:2 * D_CONV]
        h_c = p_cv[:, 2 * D_CONV:3 * D_CONV]
        g_c = p_cv[:, 3 * D_CONV:4 * D_CONV]
        u_c = c_g * h_c
        carry = carry_u[...]
        conv = cw0 * _shift_rows(u_c, carry, 2) + cw1 * _shift_rows(u_c, carry, 1) + cw2 * u_c
        carry_u[...] = u_c[CHUNK - SUBLANES:, :]
        e_yb[sel] = (b_g * conv * _silu(g_c)).astype(e_yb.dtype)

    t_i = lax.broadcasted_iota(jnp.int32, (CHUNK, LANES), 0)
    j_i = lax.broadcasted_iota(jnp.int32, (CHUNK, LANES), 1) & (HEAD - 1)

    def splits(xs):
        return [_Split(x) for x in xs]

    def pmm(xs, ys):
        return [_mm([(x, _BlockDiag(y))]) for x, y in zip(xs, ys)]

    def inverse(sel):
        strict = j_i < t_i
        incl = j_i <= t_i
        eye = (j_i == t_i).astype(F32)

        def same_block(log2_size):
            return (j_i >> log2_size) == (t_i >> log2_size)

        g = []
        for p in ps:
            lhs = jnp.concatenate([s_at[sel, p], s_rt[sel, p]], axis=0)
            rhs_t = jnp.concatenate([_BlockDiag(_Split(s_bh[sel, p])).hi,
                                     _BlockDiag(_Split(s_kh[sel, p])).hi], axis=0)
            g.append(lax.dot_general(lhs, rhs_t, (((1,), (1,)), ((), ())), preferred_element_type=F32))
        t_ab = [jnp.where(strict, g_[:CHUNK, :LANES], 0.0) for g_ in g]
        for p in ps:
            r_arb[sel, p] = jnp.where(incl, g[p][CHUNK:, :LANES], 0.0).astype(r_arb.dtype)
            r_tak[sel, p] = jnp.where(strict, g[p][:CHUNK, LANES:], 0.0).astype(r_tak.dtype)
            r_ark[sel, p] = jnp.where(incl, g[p][CHUNK:, LANES:], 0.0).astype(r_ark.dtype)
        yield

        x = splits([eye - jnp.where(same_block(1), t, 0.0) for t in t_ab])
        for log2_size in range(2, 7):
            level = same_block(log2_size) & jnp.logical_not(same_block(log2_size - 1))
            c = splits([jnp.where(level, t, 0.0) for t in t_ab])
            xc = splits(pmm(x, c))
            yield
            x = splits([x_.x - m for x_, m in zip(x, pmm(xc, x))])
            if log2_size < 6:
                yield
        for p in ps:
            r_x[sel, p] = x[p].hi

    def advance(sel):
        row = lax.broadcasted_iota(jnp.int32, (LANES, LANES), 0)
        col = lax.broadcasted_iota(jnp.int32, (LANES, LANES), 1)
        same_head = (row < HEAD) == (col < HEAD)
        vv = [s_v[sel, p] for p in ps]
        vbd = [_BlockDiag(_Split(v_)) for v_ in vv]
        s0 = [state[p] for p in ps]
        h0 = splits([s.T for s in s0])
        rhs = splits([_mm([(_Split(s_at[sel, p]), h0[p]), (_Split(r_tak[sel, p]), vbd[p])]) for p in ps])
        yield
        u = splits(pmm(splits([r_x[sel, p] for p in ps]), rhs))
        yield
        for p in ps:
            s_y[sel, p] = _mm([(_Split(s_rt[sel, p]), h0[p]), (_Split(r_ark[sel, p]), vbd[p]),
                               (_Split(-r_arb[sel, p]), _BlockDiag(u[p]))])
        yield
        for p in ps:
            wv_t = _Split(jnp.concatenate([vv[p], u[p].x], axis=0).T)
            upd = _mm([(wv_t, _Split(jnp.concatenate([s_kd[sel, p], -s_bd[sel, p]], axis=0)))])
            state[p] = jnp.where(same_head, s0[p] * s_pe[sel, p][0:1, :] + upd, 0.0)

    def finish(sel, row0):
        y = jnp.concatenate([s_y[sel, p] for p in ps], axis=1)
        inv_n = 1.0 / HEAD
        yield
        mean = seg_sums([y])[0] * inv_n
        yc = y - mean
        yield
        var = seg_sums([yc * yc])[0] * inv_n
        yn = yc * lax.rsqrt(var + LNX_EPS) * lnx_g + lnx_b
        y_a = (yn + e_bonus[sel]) * e_gate[sel]
        o_ref[0, row0:row0 + CHUNK, :D_RWKV] = y_a.astype(o_ref.dtype)
        o_ref[0, row0:row0 + CHUNK, D_RWKV:] = e_yb[sel]

    @pl.when(pl.program_id(1) == 0)
    def _():
        for ref in (carry_rw, carry_u, state, s_at, s_rt, s_v, s_kd, s_bd, s_pe,
                    r_x, r_tak, r_ark, r_arb, e_bonus, e_gate, e_yb):
            ref[...] = jnp.zeros_like(ref)
        for _ in prep(pm_ref, lm_ref, 0, 0):
            pass

    _interleave(_HALF_ORDER, {"advance": advance(1), "finish": finish(1, 0), "inverse": inverse(0),
                              "prep": prep(px_ref, lx_ref, 0, 1)})
    _interleave(_HALF_ORDER, {"advance": advance(0), "finish": finish(0, CHUNK), "inverse": inverse(1),
                              "prep": prep(px_ref, lx_ref, CHUNK, 0)})


def _mixer(proj_meta, lora_meta, proj, lora, mu, vecs, w_lora, seg_ones, tri2):
    bsz, seq, _ = proj.shape
    assert seq % (2 * CHUNK) == 0
    n_steps = seq // (2 * CHUNK) + 1
    pair_buf = pltpu.VMEM((2, N_PAIRS, CHUNK, LANES), F32)
    pair_bf16 = pltpu.VMEM((2, N_PAIRS, CHUNK, LANES), BF16)
    row_buf = pltpu.VMEM((2, CHUNK, D_RWKV), F32)
    const = lambda shape: pl.BlockSpec(shape, lambda b, j: (0,) * len(shape))
    return pl.pallas_call(
        _mixer_body,
        grid=(bsz, n_steps),
        in_specs=[
            const(proj_meta.shape), const(lora_meta.shape),
            pl.BlockSpec((1, 2 * CHUNK, MAIN_COLS), lambda b, j: (b, jnp.minimum(j, n_steps - 2), 0)),
            pl.BlockSpec((1, 2 * CHUNK, LORA_COLS), lambda b, j: (b, jnp.minimum(j, n_steps - 2), 0)),
            const(mu.shape), const(vecs.shape), const(w_lora.shape),
            const(seg_ones.shape), const(tri2.shape),
        ],
        out_specs=pl.BlockSpec((1, 2 * CHUNK, D_MODEL), lambda b, j: (b, jnp.maximum(j - 1, 0), 0)),
        out_shape=jax.ShapeDtypeStruct((bsz, seq, D_MODEL), BF16),
        scratch_shapes=[
            pltpu.VMEM((SUBLANES, RW_COLS), F32),
            pltpu.VMEM((SUBLANES, D_CONV), F32),
            pltpu.VMEM((N_PAIRS, LANES, LANES), F32),
            pair_bf16, pair_bf16, pair_bf16, pair_bf16, pair_buf, pair_bf16, pair_bf16,
            pltpu.VMEM((2, N_PAIRS, SUBLANES, LANES), F32),
            pair_buf,
            pair_bf16, pair_bf16, pair_bf16, pair_bf16,
            row_buf, row_buf,
            pltpu.VMEM((2, CHUNK, D_CONV), BF16),
        ],
        compiler_params=pltpu.CompilerParams(
            dimension_semantics=("arbitrary", "arbitrary"),
            vmem_limit_bytes=VMEM_LIMIT),
        name="mixer",
    )(proj_meta, lora_meta, proj, lora, mu, vecs, w_lora, seg_ones, tri2)


def _outproj_body(mix_ref, x_ref, w_ref, g_ref, o_ref):
    h = x_ref[...] + jnp.dot(mix_ref[...], w_ref[...], preferred_element_type=F32)
    ms = jnp.mean(h * h, axis=-1, keepdims=True)
    o_ref[...] = h * lax.rsqrt(ms + RMS_EPS) * g_ref[...]


def _outproj(mix2d, x2d, w_bf16, g):
    rows = x2d.shape[0]
    tm = OUTPROJ_TM
    assert rows % tm == 0
    return pl.pallas_call(
        _outproj_body,
        grid=(rows // tm,),
        in_specs=[
            pl.BlockSpec((tm, D_MODEL), lambda i: (i, 0)),
            pl.BlockSpec((tm, D_MODEL), lambda i: (i, 0)),
            pl.BlockSpec((D_MODEL, D_MODEL), lambda i: (0, 0)),
            pl.BlockSpec((1, D_MODEL), lambda i: (0, 0)),
        ],
        out_specs=pl.BlockSpec((tm, D_MODEL), lambda i: (i, 0)),
        out_shape=jax.ShapeDtypeStruct((rows, D_MODEL), F32),
        compiler_params=pltpu.CompilerParams(
            dimension_semantics=("arbitrary",),
            vmem_limit_bytes=VMEM_LIMIT),
        name="outproj",
    )(mix2d, x2d, w_bf16, g)


def kernel(x, meta_tokens, norm_in_g, w_in, mu_shift, w0, w_lora_up, a0, a_lora_up, k_k, k_a, r_k, lnx_g, lnx_b, conv_w, w_out, norm_f_g):
    bsz, seq, _ = x.shape
    assert norm_in_g.shape[0] == 1 and seq % CHUNK == 0
    x2d = x.reshape(bsz * seq, D_MODEL)

    front = jnp.concatenate([jnp.zeros((FRONT_PAD, D_MODEL), x.dtype), meta_tokens.astype(x.dtype)], axis=0)
    g_in = norm_in_g[0][None, :]
    w_main = jnp.concatenate([w_in[0][:, :4 * D_RWKV], w_in[0][:, RW_COLS:]], axis=1).astype(BF16)
    w_lora_in = w_in[0][:, 4 * D_RWKV:RW_COLS].astype(BF16)
    proj_meta, lora_meta = _inproj_meta(front, g_in, w_main, w_lora_in)
    proj, lora = _inproj(x, g_in, w_main, w_lora_in)

    w_lora = jnp.zeros((2 * D_LORA, 2 * D_RWKV), F32)
    w_lora = w_lora.at[:D_LORA, :D_RWKV].set(w_lora_up[0]).at[D_LORA:, D_RWKV:].set(a_lora_up[0])
    vecs = jnp.concatenate([
        w0[0][None], a0[0][None], k_k[0][None], k_a[0][None], r_k[0].reshape(1, D_RWKV),
        lnx_g[0][None], lnx_b[0][None], conv_w[0],
        jnp.zeros((16 - 10, D_RWKV), F32)], axis=0)
    lane_head = jnp.arange(LANES) // HEAD
    seg_ones = (lane_head[:, None] == lane_head[None, :]).astype(BF16)
    t_idx = jnp.arange(CHUNK)
    tri = (t_idx[None, :] <= t_idx[:, None]).astype(BF16)
    tri2 = jnp.concatenate([tri, tri], axis=1)

    mix = _mixer(proj_meta, lora_meta, proj, lora, mu_shift, vecs, w_lora.astype(BF16), seg_ones, tri2)

    out = _outproj(mix.reshape(bsz * seq, D_MODEL), x2d, w_out[0].astype(BF16), norm_f_g[None, :])
    return out.reshape(bsz, seq, D_MODEL)
```

```python
import math

import jax
import jax.numpy as jnp
from jax import lax
from jax.experimental import pallas as pl
from jax.experimental.pallas import tpu as pltpu

F32 = jnp.float32
BF16 = jnp.bfloat16

D_MODEL = 2048
N_META = 16
D_RWKV = 1024
HEAD = 64
N_HEADS = D_RWKV // HEAD
D_LORA = 64
D_CONV = 1024
RW_COLS = 4 * D_RWKV + 2 * D_LORA
IN_COLS = RW_COLS + 4 * D_CONV
RMS_EPS = 1e-6
LNX_EPS = 64e-5

CHUNK = 64
LANES = 128
SUBLANES = 8
N_PAIRS = D_RWKV // LANES
FRONT_PAD = CHUNK - N_META

VMEM_LIMIT = 56 * 1024 * 1024
INPROJ_VMEM_LIMIT = 60 * 1024 * 1024

MXU_TILE = 256
INPROJ_TM = 1024
INPROJ_TN = 8 * MXU_TILE
INPROJ_TAIL = IN_COLS % INPROJ_TN
META_TN = 1664
OUTPROJ_TM = 512


def _rmsnorm_bf16(x, g):
    ms = jnp.mean(x * x, axis=-1, keepdims=True)
    return (x * lax.rsqrt(ms + RMS_EPS) * g).astype(BF16)


def _inproj_body(x_ref, g_ref, w_ref, o_ref, hn_ref):
    j = pl.program_id(2)
    n_full = IN_COLS // INPROJ_TN

    @pl.when(j == 0)
    def _():
        hn_ref[...] = _rmsnorm_bf16(x_ref[0], g_ref[...])

    @pl.when(j < n_full)
    def _():
        o_ref[0] = jnp.dot(hn_ref[...], w_ref[...], preferred_element_type=F32)

    @pl.when(j == n_full)
    def _():
        o_ref[0, :, :INPROJ_TAIL] = jnp.dot(hn_ref[...], w_ref[:, :INPROJ_TAIL], preferred_element_type=F32)


def _inproj(x, g, w_bf16):
    bsz, seq, _ = x.shape
    tm, tn = INPROJ_TM, INPROJ_TN
    assert seq % tm == 0 and INPROJ_TAIL % LANES == 0
    return pl.pallas_call(
        _inproj_body,
        grid=(bsz, seq // tm, pl.cdiv(IN_COLS, tn)),
        in_specs=[
            pl.BlockSpec((1, tm, D_MODEL), lambda b, i, j: (b, i, 0)),
            pl.BlockSpec((1, D_MODEL), lambda b, i, j: (0, 0)),
            pl.BlockSpec((D_MODEL, tn), lambda b, i, j: (0, j)),
        ],
        out_specs=pl.BlockSpec((1, tm, tn), lambda b, i, j: (b, i, j)),
        out_shape=jax.ShapeDtypeStruct((bsz, seq, IN_COLS), F32),
        scratch_shapes=[pltpu.VMEM((tm, D_MODEL), BF16)],
        compiler_params=pltpu.CompilerParams(
            dimension_semantics=("arbitrary", "arbitrary", "arbitrary"),
            vmem_limit_bytes=INPROJ_VMEM_LIMIT),
        name="inproj",
    )(x, g, w_bf16)


def _inproj_meta_body(x_ref, g_ref, w_ref, o_ref):
    o_ref[0] = jnp.dot(_rmsnorm_bf16(x_ref[...], g_ref[...]), w_ref[...], preferred_element_type=F32)


def _inproj_meta(front, g, w_bf16):
    tn = META_TN
    assert IN_COLS % tn == 0
    return pl.pallas_call(
        _inproj_meta_body,
        grid=(IN_COLS // tn,),
        in_specs=[
            pl.BlockSpec((CHUNK, D_MODEL), lambda j: (0, 0)),
            pl.BlockSpec((1, D_MODEL), lambda j: (0, 0)),
            pl.BlockSpec((D_MODEL, tn), lambda j: (0, j)),
        ],
        out_specs=pl.BlockSpec((1, CHUNK, tn), lambda j: (0, 0, j)),
        out_shape=jax.ShapeDtypeStruct((1, CHUNK, IN_COLS), F32),
        compiler_params=pltpu.CompilerParams(
            dimension_semantics=("arbitrary",),
            vmem_limit_bytes=VMEM_LIMIT),
        name="inproj_meta",
    )(front, g, w_bf16)


class _Split:
    def __init__(self, x):
        self.x = x
        self._hi = None
        self._lo = None

    @property
    def hi(self):
        if self._hi is None:
            self._hi = self.x.astype(BF16)
        return self._hi

    @property
    def lo(self):
        if self._lo is None:
            self._lo = (self.x - self.hi.astype(F32)).astype(BF16)
        return self._lo


class _BlockDiag:
    def __init__(self, y):
        self._y = y
        self._hi = None

    @property
    def hi(self):
        if self._hi is None:
            yp = self._y.hi
            lane = lax.broadcasted_iota(jnp.int32, yp.shape, 1)
            first = lane < HEAD
            zero = jnp.zeros_like(yp)
            self._hi = jnp.concatenate([jnp.where(first, yp, zero), jnp.where(first, zero, yp)], axis=0)
        return self._hi


def _mm(terms):
    la = jnp.concatenate([a.hi for a, _ in terms], axis=1)
    lb = jnp.concatenate([b.hi for _, b in terms], axis=0)
    return jnp.dot(la, lb, preferred_element_type=F32)


def _shift_rows(x, carry, n):
    row = lax.broadcasted_iota(jnp.int32, x.shape, 0)
    out = pltpu.roll(x, n, axis=0)
    for i in range(n):
        src = SUBLANES - n + i
        out = jnp.where(row == i, carry[src:src + 1, :], out)
    return out


def _sigmoid(x):
    return 0.5 + 0.5 * jnp.tanh(0.5 * x)


def _silu(x):
    h = 0.5 * x
    return h + h * jnp.tanh(h)


def _interleave(order, stages):
    for name in order:
        next(stages[name], None)
    for gen in stages.values():
        for _ in gen:
            pass


_HALF_ORDER = ("advance", "inverse", "prep", "advance", "inverse", "advance", "inverse", "finish", "advance",
               "prep", "finish", "inverse", "inverse", "prep", "finish", "inverse", "inverse", "prep",
               "inverse", "inverse", "inverse", "inverse")


def _mixer_body(pm_ref, px_ref, mu_ref, vec_ref, wl_ref, ones_ref, tri_ref, o_ref,
                carry_rw, carry_u, state, s_at, s_rt, s_kh, s_bh, s_v, s_kd, s_bd, s_pe, s_y,
                r_x, r_tak, r_ark, r_arb, e_bonus, e_gate, e_yb):
    def vec(i):
        return vec_ref[i:i + 1, :]

    w0, a0, k_k, k_a, r_k, lnx_g, lnx_b, cw0, cw1, cw2 = (vec(i) for i in range(10))
    ps = range(N_PAIRS)

    def seg_sums(xs):
        tiles = [x[:, p * LANES:(p + 1) * LANES] for x in xs for p in ps]
        t = jnp.concatenate(tiles, axis=0).astype(BF16)
        s = jnp.dot(t, ones_ref[...], preferred_element_type=F32)
        out = []
        for i in range(len(xs)):
            rows = [s[(i * N_PAIRS + p) * CHUNK:(i * N_PAIRS + p + 1) * CHUNK] for p in ps]
            out.append(jnp.concatenate(rows, axis=1))
        return out

    def prep(p_ref, row0, sel):
        p_rw = p_ref[0, row0:row0 + CHUNK, :RW_COLS]
        prev = _shift_rows(p_rw, carry_rw[...], 1)
        carry_rw[...] = p_rw[CHUNK - SUBLANES:, :]
        pm = p_rw + (prev - p_rw) * mu_ref[...]
        r = pm[:, 0 * D_RWKV:1 * D_RWKV]
        k = pm[:, 1 * D_RWKV:2 * D_RWKV]
        v = pm[:, 2 * D_RWKV:3 * D_RWKV]
        g_r = pm[:, 3 * D_RWKV:4 * D_RWKV]
        lora_in = pm[:, 4 * D_RWKV:]
        lane = lax.broadcasted_iota(jnp.int32, lora_in.shape, 1)
        lora_in = jnp.where(lane < D_LORA, jnp.tanh(lora_in), lora_in).astype(BF16)
        yield
        lora = jnp.dot(lora_in, wl_ref[...], preferred_element_type=F32)
        logw = (-math.exp(-0.5)) * _sigmoid(w0 + lora[:, :D_RWKV])
        a = _sigmoid(a0 + lora[:, D_RWKV:])
        yield
        logw = _Split(logw)
        cum = jnp.dot(tri_ref[...], jnp.concatenate([logw.hi, logw.lo], axis=0),
                      preferred_element_type=F32)
        logw = logw.x
        kk = k * k_k
        k = k * (1.0 + (a - 1.0) * k_a)
        yield
        kk_sq, rk_sum = seg_sums([kk * kk, r * k * r_k])
        kk = kk * lax.rsqrt(jnp.maximum(kk_sq, 1e-24))
        beta = kk * a
        e_in = jnp.exp(cum)
        e_out = jnp.exp(-cum)
        e_ex = jnp.exp(cum - logw)
        p_end = e_in[CHUNK - 1:CHUNK, :]
        a_t = kk * e_ex
        r_t = r * e_in
        k_h = k * e_out
        b_h = beta * e_out
        k_d = k_h * p_end
        b_d = b_h * p_end
        for dst, val in ((s_at, a_t), (s_rt, r_t), (s_kh, k_h), (s_bh, b_h), (s_v, v), (s_kd, k_d), (s_bd, b_d)):
            val = val.astype(dst.dtype)
            for p in ps:
                dst[sel, p] = val[:, p * LANES:(p + 1) * LANES]
        for p in ps:
            s_pe[sel, p] = jnp.broadcast_to(p_end[:, p * LANES:(p + 1) * LANES], (SUBLANES, LANES))
        e_bonus[sel] = rk_sum * v
        e_gate[sel] = _silu(g_r)

        p_cv = p_ref[0, row0:row0 + CHUNK, RW_COLS:]
        b_g = p_cv[:, 0 * D_CONV:1 * D_CONV]
        c_g = p_cv[:, 1 * D_CONV:2 * D_CONV]
        h_c = p_cv[:, 2 * D_CONV:3 * D_CONV]
        g_c = p_cv[:, 3 * D_CONV:4 * D_CONV]
        u_c = c_g * h_c
        carry = carry_u[...]
        conv = cw0 * _shift_rows(u_c, carry, 2) + cw1 * _shift_rows(u_c, carry, 1) + cw2 * u_c
        carry_u[...] = u_c[CHUNK - SUBLANES:, :]
        e_yb[sel] = (b_g * conv * _silu(g_c)).astype(e_yb.dtype)

    t_i = lax.broadcasted_iota(jnp.int32, (CHUNK, LANES), 0)
    j_i = lax.broadcasted_iota(jnp.int32, (CHUNK, LANES), 1) & (HEAD - 1)

    def splits(xs):
        return [_Split(x) for x in xs]

    def pmm(xs, ys):
        return [_mm([(x, _BlockDiag(y))]) for x, y in zip(xs, ys)]

    def inverse(sel):
        strict = j_i < t_i
        incl = j_i <= t_i
        eye = (j_i == t_i).astype(F32)

        def same_block(log2_size):
            return (j_i >> log2_size) == (t_i >> log2_size)

        g = []
        for p in ps:
            lhs = jnp.concatenate([s_at[sel, p], s_rt[sel, p]], axis=0)
            rhs_t = jnp.concatenate([_BlockDiag(_Split(s_bh[sel, p])).hi,
                                     _BlockDiag(_Split(s_kh[sel, p])).hi], axis=0)
            g.append(lax.dot_general(lhs, rhs_t, (((1,), (1,)), ((), ())), preferred_element_type=F32))
        t_ab = [jnp.where(strict, g_[:CHUNK, :LANES], 0.0) for g_ in g]
        for p in ps:
            r_arb[sel, p] = jnp.where(incl, g[p][CHUNK:, :LANES], 0.0).astype(r_arb.dtype)
            r_tak[sel, p] = jnp.where(strict, g[p][:CHUNK, LANES:], 0.0).astype(r_tak.dtype)
            r_ark[sel, p] = jnp.where(incl, g[p][CHUNK:, LANES:], 0.0).astype(r_ark.dtype)
        yield

        x = splits([eye - jnp.where(same_block(1), t, 0.0) for t in t_ab])
        for log2_size in range(2, 7):
            level = same_block(log2_size) & jnp.logical_not(same_block(log2_size - 1))
            c = splits([jnp.where(level, t, 0.0) for t in t_ab])
            xc = splits(pmm(x, c))
            yield
            x = splits([x_.x - m for x_, m in zip(x, pmm(xc, x))])
            if log2_size < 6:
                yield
        for p in ps:
            r_x[sel, p] = x[p].hi

    def advance(sel):
        row = lax.broadcasted_iota(jnp.int32, (LANES, LANES), 0)
        col = lax.broadcasted_iota(jnp.int32, (LANES, LANES), 1)
        same_head = (row < HEAD) == (col < HEAD)
        vv = [s_v[sel, p] for p in ps]
        vbd = [_BlockDiag(_Split(v_)) for v_ in vv]
        s0 = [state[p] for p in ps]
        h0 = splits([s.T for s in s0])
        rhs = splits([_mm([(_Split(s_at[sel, p]), h0[p]), (_Split(r_tak[sel, p]), vbd[p])]) for p in ps])
        yield
        u = splits(pmm(splits([r_x[sel, p] for p in ps]), rhs))
        yield
        for p in ps:
            s_y[sel, p] = _mm([(_Split(s_rt[sel, p]), h0[p]), (_Split(r_ark[sel, p]), vbd[p]),
                               (_Split(-r_arb[sel, p]), _BlockDiag(u[p]))])
        yield
        for p in ps:
            wv_t = _Split(jnp.concatenate([vv[p], u[p].x], axis=0).T)
            upd = _mm([(wv_t, _Split(jnp.concatenate([s_kd[sel, p], -s_bd[sel, p]], axis=0)))])
            state[p] = jnp.where(same_head, s0[p] * s_pe[sel, p][0:1, :] + upd, 0.0)

    def finish(sel, row0):
        y = jnp.concatenate([s_y[sel, p] for p in ps], axis=1)
        inv_n = 1.0 / HEAD
        yield
        mean = seg_sums([y])[0] * inv_n
        yc = y - mean
        yield
        var = seg_sums([yc * yc])[0] * inv_n
        yn = yc * lax.rsqrt(var + LNX_EPS) * lnx_g + lnx_b
        y_a = (yn + e_bonus[sel]) * e_gate[sel]
        o_ref[0, row0:row0 + CHUNK, :D_RWKV] = y_a.astype(o_ref.dtype)
        o_ref[0, row0:row0 + CHUNK, D_RWKV:] = e_yb[sel]

    @pl.when(pl.program_id(1) == 0)
    def _():
        for ref in (carry_rw, carry_u, state, s_at, s_rt, s_v, s_kd, s_bd, s_pe,
                    r_x, r_tak, r_ark, r_arb, e_bonus, e_gate, e_yb):
            ref[...] = jnp.zeros_like(ref)
        for _ in prep(pm_ref, 0, 0):
            pass

    _interleave(_HALF_ORDER, {"advance": advance(1), "finish": finish(1, 0), "inverse": inverse(0),
                              "prep": prep(px_ref, 0, 1)})
    _interleave(_HALF_ORDER, {"advance": advance(0), "finish": finish(0, CHUNK), "inverse": inverse(1),
                              "prep": prep(px_ref, CHUNK, 0)})


def _mixer(proj_meta, proj, mu, vecs, w_lora, seg_ones, tri2):
    bsz, seq, _ = proj.shape
    assert seq % (2 * CHUNK) == 0
    n_steps = seq // (2 * CHUNK) + 1
    pair_buf = pltpu.VMEM((2, N_PAIRS, CHUNK, LANES), F32)
    pair_bf16 = pltpu.VMEM((2, N_PAIRS, CHUNK, LANES), BF16)
    row_buf = pltpu.VMEM((2, CHUNK, D_RWKV), F32)
    const = lambda shape: pl.BlockSpec(shape, lambda b, j: (0,) * len(shape))
    return pl.pallas_call(
        _mixer_body,
        grid=(bsz, n_steps),
        in_specs=[
            const(proj_meta.shape),
            pl.BlockSpec((1, 2 * CHUNK, IN_COLS), lambda b, j: (b, jnp.minimum(j, n_steps - 2), 0)),
            const(mu.shape), const(vecs.shape), const(w_lora.shape),
            const(seg_ones.shape), const(tri2.shape),
        ],
        out_specs=pl.BlockSpec((1, 2 * CHUNK, D_MODEL), lambda b, j: (b, jnp.maximum(j - 1, 0), 0)),
        out_shape=jax.ShapeDtypeStruct((bsz, seq, D_MODEL), BF16),
        scratch_shapes=[
            pltpu.VMEM((SUBLANES, RW_COLS), F32),
            pltpu.VMEM((SUBLANES, D_CONV), F32),
            pltpu.VMEM((N_PAIRS, LANES, LANES), F32),
            pair_bf16, pair_bf16, pair_bf16, pair_bf16, pair_buf, pair_bf16, pair_bf16,
            pltpu.VMEM((2, N_PAIRS, SUBLANES, LANES), F32),
            pair_buf,
            pair_bf16, pair_bf16, pair_bf16, pair_bf16,
            row_buf, row_buf,
            pltpu.VMEM((2, CHUNK, D_CONV), BF16),
        ],
        compiler_params=pltpu.CompilerParams(
            dimension_semantics=("arbitrary", "arbitrary"),
            vmem_limit_bytes=VMEM_LIMIT),
        name="mixer",
    )(proj_meta, proj, mu, vecs, w_lora, seg_ones, tri2)


def _outproj_body(mix_ref, x_ref, w_ref, g_ref, o_ref):
    h = x_ref[...] + jnp.dot(mix_ref[...], w_ref[...], preferred_element_type=F32)
    ms = jnp.mean(h * h, axis=-1, keepdims=True)
    o_ref[...] = h * lax.rsqrt(ms + RMS_EPS) * g_ref[...]


def _outproj(mix2d, x2d, w_bf16, g):
    rows = x2d.shape[0]
    tm = OUTPROJ_TM
    assert rows % tm == 0
    return pl.pallas_call(
        _outproj_body,
        grid=(rows // tm,),
        in_specs=[
            pl.BlockSpec((tm, D_MODEL), lambda i: (i, 0)),
            pl.BlockSpec((tm, D_MODEL), lambda i: (i, 0)),
            pl.BlockSpec((D_MODEL, D_MODEL), lambda i: (0, 0)),
            pl.BlockSpec((1, D_MODEL), lambda i: (0, 0)),
        ],
        out_specs=pl.BlockSpec((tm, D_MODEL), lambda i: (i, 0)),
        out_shape=jax.ShapeDtypeStruct((rows, D_MODEL), F32),
        compiler_params=pltpu.CompilerParams(
            dimension_semantics=("arbitrary",),
            vmem_limit_bytes=VMEM_LIMIT),
        name="outproj",
    )(mix2d, x2d, w_bf16, g)


def kernel(x, meta_tokens, norm_in_g, w_in, mu_shift, w0, w_lora_up, a0, a_lora_up, k_k, k_a, r_k, lnx_g, lnx_b, conv_w, w_out, norm_f_g):
    bsz, seq, _ = x.shape
    assert norm_in_g.shape[0] == 1 and seq % CHUNK == 0
    x2d = x.reshape(bsz * seq, D_MODEL)

    front = jnp.concatenate([jnp.zeros((FRONT_PAD, D_MODEL), x.dtype), meta_tokens.astype(x.dtype)], axis=0)
    g_in = norm_in_g[0][None, :]
    w_in_bf16 = w_in[0].astype(BF16)
    proj_meta = _inproj_meta(front, g_in, w_in_bf16)
    proj = _inproj(x, g_in, w_in_bf16)

    w_lora = jnp.zeros((2 * D_LORA, 2 * D_RWKV), F32)
    w_lora = w_lora.at[:D_LORA, :D_RWKV].set(w_lora_up[0]).at[D_LORA:, D_RWKV:].set(a_lora_up[0])
    vecs = jnp.concatenate([
        w0[0][None], a0[0][None], k_k[0][None], k_a[0][None], r_k[0].reshape(1, D_RWKV),
        lnx_g[0][None], lnx_b[0][None], conv_w[0],
        jnp.zeros((16 - 10, D_RWKV), F32)], axis=0)
    lane_head = jnp.arange(LANES) // HEAD
    seg_ones = (lane_head[:, None] == lane_head[None, :]).astype(BF16)
    t_idx = jnp.arange(CHUNK)
    tri = (t_idx[None, :] <= t_idx[:, None]).astype(BF16)
    tri2 = jnp.concatenate([tri, tri], axis=1)

    mix = _mixer(proj_meta, proj, mu_shift, vecs, w_lora.astype(BF16), seg_ones, tri2)

    out = _outproj(mix.reshape(bsz * seq, D_MODEL), x2d, w_out[0].astype(BF16), norm_f_g[None, :])
    return out.reshape(bsz, seq, D_MODEL)
```

```python
import math

import jax
import jax.numpy as jnp
from jax import lax
from jax.experimental import pallas as pl
from jax.experimental.pallas import tpu as pltpu

F32 = jnp.float32
BF16 = jnp.bfloat16

D_MODEL = 2048
N_META = 16
D_RWKV = 1024
HEAD = 64
N_HEADS = D_RWKV // HEAD
D_LORA = 64
D_CONV = 1024
RW_COLS = 4 * D_RWKV + 2 * D_LORA
IN_COLS = RW_COLS + 4 * D_CONV
RMS_EPS = 1e-6
LNX_EPS = 64e-5

CHUNK = 64
LANES = 128
SUBLANES = 8
N_PAIRS = D_RWKV // LANES
FRONT_PAD = CHUNK - N_META

VMEM_LIMIT = 56 * 1024 * 1024
INPROJ_VMEM_LIMIT = 60 * 1024 * 1024

MXU_TILE = 256
INPROJ_TM = 1024
INPROJ_TN = 8 * MXU_TILE
INPROJ_TAIL = IN_COLS % INPROJ_TN
META_TN = 1664
OUTPROJ_TM = 512


def _rmsnorm_bf16(x, g):
    ms = jnp.mean(x * x, axis=-1, keepdims=True)
    return (x * lax.rsqrt(ms + RMS_EPS) * g).astype(BF16)


def _inproj_col_block(j):
    return jnp.where(j == 0, IN_COLS // INPROJ_TN, j - 1)


def _inproj_body(x_ref, g_ref, w_ref, o_ref, hn_ref):
    @pl.when(pl.program_id(2) == 0)
    def _():
        hn_ref[...] = _rmsnorm_bf16(x_ref[0], g_ref[...])
        o_ref[0, :, :INPROJ_TAIL] = jnp.dot(hn_ref[...], w_ref[:, :INPROJ_TAIL], preferred_element_type=F32)

    @pl.when(pl.program_id(2) > 0)
    def _():
        o_ref[0] = jnp.dot(hn_ref[...], w_ref[...], preferred_element_type=F32)


def _inproj(x, g, w_bf16):
    bsz, seq, _ = x.shape
    tm, tn = INPROJ_TM, INPROJ_TN
    assert seq % tm == 0 and INPROJ_TAIL % LANES == 0
    return pl.pallas_call(
        _inproj_body,
        grid=(bsz, seq // tm, pl.cdiv(IN_COLS, tn)),
        in_specs=[
            pl.BlockSpec((1, tm, D_MODEL), lambda b, i, j: (b, i, 0)),
            pl.BlockSpec((1, D_MODEL), lambda b, i, j: (0, 0)),
            pl.BlockSpec((D_MODEL, tn), lambda b, i, j: (0, _inproj_col_block(j))),
        ],
        out_specs=pl.BlockSpec((1, tm, tn), lambda b, i, j: (b, i, _inproj_col_block(j))),
        out_shape=jax.ShapeDtypeStruct((bsz, seq, IN_COLS), F32),
        scratch_shapes=[pltpu.VMEM((tm, D_MODEL), BF16)],
        compiler_params=pltpu.CompilerParams(
            dimension_semantics=("arbitrary", "arbitrary", "arbitrary"),
            vmem_limit_bytes=INPROJ_VMEM_LIMIT),
        name="inproj",
    )(x, g, w_bf16)


def _inproj_meta_body(x_ref, g_ref, w_ref, o_ref):
    o_ref[0] = jnp.dot(_rmsnorm_bf16(x_ref[...], g_ref[...]), w_ref[...], preferred_element_type=F32)


def _inproj_meta(front, g, w_bf16):
    tn = META_TN
    assert IN_COLS % tn == 0
    return pl.pallas_call(
        _inproj_meta_body,
        grid=(IN_COLS // tn,),
        in_specs=[
            pl.BlockSpec((CHUNK, D_MODEL), lambda j: (0, 0)),
            pl.BlockSpec((1, D_MODEL), lambda j: (0, 0)),
            pl.BlockSpec((D_MODEL, tn), lambda j: (0, j)),
        ],
        out_specs=pl.BlockSpec((1, CHUNK, tn), lambda j: (0, 0, j)),
        out_shape=jax.ShapeDtypeStruct((1, CHUNK, IN_COLS), F32),
        compiler_params=pltpu.CompilerParams(
            dimension_semantics=("arbitrary",),
            vmem_limit_bytes=VMEM_LIMIT),
        name="inproj_meta",
    )(front, g, w_bf16)


class _Split:
    def __init__(self, x):
        self.x = x
        self._hi = None
        self._lo = None

    @property
    def hi(self):
        if self._hi is None:
            self._hi = self.x.astype(BF16)
        return self._hi

    @property
    def lo(self):
        if self._lo is None:
            self._lo = (self.x - self.hi.astype(F32)).astype(BF16)
        return self._lo


class _BlockDiag:
    def __init__(self, y):
        self._y = y
        self._hi = None

    @property
    def hi(self):
        if self._hi is None:
            yp = self._y.hi
            lane = lax.broadcasted_iota(jnp.int32, yp.shape, 1)
            first = lane < HEAD
            zero = jnp.zeros_like(yp)
            self._hi = jnp.concatenate([jnp.where(first, yp, zero), jnp.where(first, zero, yp)], axis=0)
        return self._hi


def _mm(terms):
    la = jnp.concatenate([a.hi for a, _ in terms], axis=1)
    lb = jnp.concatenate([b.hi for _, b in terms], axis=0)
    return jnp.dot(la, lb, preferred_element_type=F32)


def _shift_rows(x, carry, n):
    row = lax.broadcasted_iota(jnp.int32, x.shape, 0)
    out = pltpu.roll(x, n, axis=0)
    for i in range(n):
        src = SUBLANES - n + i
        out = jnp.where(row == i, carry[src:src + 1, :], out)
    return out


def _sigmoid(x):
    return 0.5 + 0.5 * jnp.tanh(0.5 * x)


def _silu(x):
    h = 0.5 * x
    return h + h * jnp.tanh(h)


def _interleave(order, stages):
    for name in order:
        next(stages[name], None)
    for gen in stages.values():
        for _ in gen:
            pass


_HALF_ORDER = ("advance", "inverse", "prep", "advance", "inverse", "advance", "inverse", "finish", "advance",
               "prep", "finish", "inverse", "inverse", "prep", "finish", "inverse", "inverse", "prep",
               "inverse", "inverse", "inverse", "inverse")


def _mixer_body(pm_ref, px_ref, mu_ref, vec_ref, wl_ref, ones_ref, tri_ref, o_ref,
                carry_rw, carry_u, state, s_at, s_rt, s_kh, s_bh, s_v, s_kd, s_bd, s_pe, s_y,
                r_x, r_tak, r_ark, r_arb, e_bonus, e_gate, e_yb):
    def vec(i):
        return vec_ref[i:i + 1, :]

    w0, a0, k_k, k_a, r_k, lnx_g, lnx_b, cw0, cw1, cw2 = (vec(i) for i in range(10))
    ps = range(N_PAIRS)

    def seg_sums(xs):
        tiles = [x[:, p * LANES:(p + 1) * LANES] for x in xs for p in ps]
        t = jnp.concatenate(tiles, axis=0).astype(BF16)
        s = jnp.dot(t, ones_ref[...], preferred_element_type=F32)
        out = []
        for i in range(len(xs)):
            rows = [s[(i * N_PAIRS + p) * CHUNK:(i * N_PAIRS + p + 1) * CHUNK] for p in ps]
            out.append(jnp.concatenate(rows, axis=1))
        return out

    def prep(p_ref, row0, sel):
        p_rw = p_ref[0, row0:row0 + CHUNK, :RW_COLS]
        prev = _shift_rows(p_rw, carry_rw[...], 1)
        carry_rw[...] = p_rw[CHUNK - SUBLANES:, :]
        pm = p_rw + (prev - p_rw) * mu_ref[...]
        r = pm[:, 0 * D_RWKV:1 * D_RWKV]
        k = pm[:, 1 * D_RWKV:2 * D_RWKV]
        v = pm[:, 2 * D_RWKV:3 * D_RWKV]
        g_r = pm[:, 3 * D_RWKV:4 * D_RWKV]
        lora_in = pm[:, 4 * D_RWKV:]
        lane = lax.broadcasted_iota(jnp.int32, lora_in.shape, 1)
        lora_in = jnp.where(lane < D_LORA, jnp.tanh(lora_in), lora_in).astype(BF16)
        yield
        lora = jnp.dot(lora_in, wl_ref[...], preferred_element_type=F32)
        logw = (-math.exp(-0.5)) * _sigmoid(w0 + lora[:, :D_RWKV])
        a = _sigmoid(a0 + lora[:, D_RWKV:])
        yield
        logw = _Split(logw)
        cum = jnp.dot(tri_ref[...], jnp.concatenate([logw.hi, logw.lo], axis=0),
                      preferred_element_type=F32)
        logw = logw.x
        kk = k * k_k
        k = k * (1.0 + (a - 1.0) * k_a)
        yield
        kk_sq, rk_sum = seg_sums([kk * kk, r * k * r_k])
        kk = kk * lax.rsqrt(jnp.maximum(kk_sq, 1e-24))
        beta = kk * a
        e_in = jnp.exp(cum)
        e_out = jnp.exp(-cum)
        e_ex = jnp.exp(cum - logw)
        p_end = e_in[CHUNK - 1:CHUNK, :]
        a_t = kk * e_ex
        r_t = r * e_in
        k_h = k * e_out
        b_h = beta * e_out
        k_d = k_h * p_end
        b_d = b_h * p_end
        for dst, val in ((s_at, a_t), (s_rt, r_t), (s_kh, k_h), (s_bh, b_h), (s_v, v), (s_kd, k_d), (s_bd, b_d)):
            val = val.astype(dst.dtype)
            for p in ps:
                dst[sel, p] = val[:, p * LANES:(p + 1) * LANES]
        for p in ps:
            s_pe[sel, p] = jnp.broadcast_to(p_end[:, p * LANES:(p + 1) * LANES], (SUBLANES, LANES))
        e_bonus[sel] = rk_sum * v
        e_gate[sel] = _silu(g_r)

        p_cv = p_ref[0, row0:row0 + CHUNK, RW_COLS:]
        b_g = p_cv[:, 0 * D_CONV:1 * D_CONV]
        c_g = p_cv[:, 1 * D_CONV:2 * D_CONV]
        h_c = p_cv[:, 2 * D_CONV:3 * D_CONV]
        g_c = p_cv[:, 3 * D_CONV:4 * D_CONV]
        u_c = c_g * h_c
        carry = carry_u[...]
        conv = cw0 * _shift_rows(u_c, carry, 2) + cw1 * _shift_rows(u_c, carry, 1) + cw2 * u_c
        carry_u[...] = u_c[CHUNK - SUBLANES:, :]
        e_yb[sel] = (b_g * conv * _silu(g_c)).astype(e_yb.dtype)

    t_i = lax.broadcasted_iota(jnp.int32, (CHUNK, LANES), 0)
    j_i = lax.broadcasted_iota(jnp.int32, (CHUNK, LANES), 1) & (HEAD - 1)

    def splits(xs):
        return [_Split(x) for x in xs]

    def pmm(xs, ys):
        return [_mm([(x, _BlockDiag(y))]) for x, y in zip(xs, ys)]

    def inverse(sel):
        strict = j_i < t_i
        incl = j_i <= t_i
        eye = (j_i == t_i).astype(F32)

        def same_block(log2_size):
            return (j_i >> log2_size) == (t_i >> log2_size)

        g = []
        for p in ps:
            lhs = jnp.concatenate([s_at[sel, p], s_rt[sel, p]], axis=0)
            rhs_t = jnp.concatenate([_BlockDiag(_Split(s_bh[sel, p])).hi,
                                     _BlockDiag(_Split(s_kh[sel, p])).hi], axis=0)
            g.append(lax.dot_general(lhs, rhs_t, (((1,), (1,)), ((), ())), preferred_element_type=F32))
        t_ab = [jnp.where(strict, g_[:CHUNK, :LANES], 0.0) for g_ in g]
        for p in ps:
            r_arb[sel, p] = jnp.where(incl, g[p][CHUNK:, :LANES], 0.0).astype(r_arb.dtype)
            r_tak[sel, p] = jnp.where(strict, g[p][:CHUNK, LANES:], 0.0).astype(r_tak.dtype)
            r_ark[sel, p] = jnp.where(incl, g[p][CHUNK:, LANES:], 0.0).astype(r_ark.dtype)
        yield

        x = splits([eye - jnp.where(same_block(1), t, 0.0) for t in t_ab])
        for log2_size in range(2, 7):
            level = same_block(log2_size) & jnp.logical_not(same_block(log2_size - 1))
            c = splits([jnp.where(level, t, 0.0) for t in t_ab])
            xc = splits(pmm(x, c))
            yield
            x = splits([x_.x - m for x_, m in zip(x, pmm(xc, x))])
            if log2_size < 6:
                yield
        for p in ps:
            r_x[sel, p] = x[p].hi

    def advance(sel):
        row = lax.broadcasted_iota(jnp.int32, (LANES, LANES), 0)
        col = lax.broadcasted_iota(jnp.int32, (LANES, LANES), 1)
        same_head = (row < HEAD) == (col < HEAD)
        vv = [s_v[sel, p] for p in ps]
        vbd = [_BlockDiag(_Split(v_)) for v_ in vv]
        s0 = [state[p] for p in ps]
        h0 = splits([s.T for s in s0])
        rhs = splits([_mm([(_Split(s_at[sel, p]), h0[p]), (_Split(r_tak[sel, p]), vbd[p])]) for p in ps])
        yield
        u = splits(pmm(splits([r_x[sel, p] for p in ps]), rhs))
        yield
        for p in ps:
            s_y[sel, p] = _mm([(_Split(s_rt[sel, p]), h0[p]), (_Split(r_ark[sel, p]), vbd[p]),
                               (_Split(-r_arb[sel, p]), _BlockDiag(u[p]))])
        yield
        for p in ps:
            wv_t = _Split(jnp.concatenate([vv[p], u[p].x], axis=0).T)
            upd = _mm([(wv_t, _Split(jnp.concatenate([s_kd[sel, p], -s_bd[sel, p]], axis=0)))])
            state[p] = jnp.where(same_head, s0[p] * s_pe[sel, p][0:1, :] + upd, 0.0)

    def finish(sel, row0):
        y = jnp.concatenate([s_y[sel, p] for p in ps], axis=1)
        inv_n = 1.0 / HEAD
        yield
        mean = seg_sums([y])[0] * inv_n
        yc = y - mean
        yield
        var = seg_sums([yc * yc])[0] * inv_n
        yn = yc * lax.rsqrt(var + LNX_EPS) * lnx_g + lnx_b
        y_a = (yn + e_bonus[sel]) * e_gate[sel]
        o_ref[0, row0:row0 + CHUNK, :D_RWKV] = y_a.astype(o_ref.dtype)
        o_ref[0, row0:row0 + CHUNK, D_RWKV:] = e_yb[sel]

    @pl.when(pl.program_id(1) == 0)
    def _():
        for ref in (carry_rw, carry_u, state, s_at, s_rt, s_v, s_kd, s_bd, s_pe,
                    r_x, r_tak, r_ark, r_arb, e_bonus, e_gate, e_yb):
            ref[...] = jnp.zeros_like(ref)
        for _ in prep(pm_ref, 0, 0):
            pass

    _interleave(_HALF_ORDER, {"advance": advance(1), "finish": finish(1, 0), "inverse": inverse(0),
                              "prep": prep(px_ref, 0, 1)})
    _interleave(_HALF_ORDER, {"advance": advance(0), "finish": finish(0, CHUNK), "inverse": inverse(1),
                              "prep": prep(px_ref, CHUNK, 0)})


def _mixer(proj_meta, proj, mu, vecs, w_lora, seg_ones, tri2):
    bsz, seq, _ = proj.shape
    assert seq % (2 * CHUNK) == 0
    n_steps = seq // (2 * CHUNK) + 1
    pair_buf = pltpu.VMEM((2, N_PAIRS, CHUNK, LANES), F32)
    pair_bf16 = pltpu.VMEM((2, N_PAIRS, CHUNK, LANES), BF16)
    row_buf = pltpu.VMEM((2, CHUNK, D_RWKV), F32)
    const = lambda shape: pl.BlockSpec(shape, lambda b, j: (0,) * len(shape))
    return pl.pallas_call(
        _mixer_body,
        grid=(bsz, n_steps),
        in_specs=[
            const(proj_meta.shape),
            pl.BlockSpec((1, 2 * CHUNK, IN_COLS), lambda b, j: (b, jnp.minimum(j, n_steps - 2), 0)),
            const(mu.shape), const(vecs.shape), const(w_lora.shape),
            const(seg_ones.shape), const(tri2.shape),
        ],
        out_specs=pl.BlockSpec((1, 2 * CHUNK, D_MODEL), lambda b, j: (b, jnp.maximum(j - 1, 0), 0)),
        out_shape=jax.ShapeDtypeStruct((bsz, seq, D_MODEL), BF16),
        scratch_shapes=[
            pltpu.VMEM((SUBLANES, RW_COLS), F32),
            pltpu.VMEM((SUBLANES, D_CONV), F32),
            pltpu.VMEM((N_PAIRS, LANES, LANES), F32),
            pair_bf16, pair_bf16, pair_bf16, pair_bf16, pair_buf, pair_bf16, pair_bf16,
            pltpu.VMEM((2, N_PAIRS, SUBLANES, LANES), F32),
            pair_buf,
            pair_bf16, pair_bf16, pair_bf16, pair_bf16,
            row_buf, row_buf,
            pltpu.VMEM((2, CHUNK, D_CONV), BF16),
        ],
        compiler_params=pltpu.CompilerParams(
            dimension_semantics=("arbitrary", "arbitrary"),
            vmem_limit_bytes=VMEM_LIMIT),
        name="mixer",
    )(proj_meta, proj, mu, vecs, w_lora, seg_ones, tri2)


def _outproj_body(mix_ref, x_ref, w_ref, g_ref, o_ref):
    h = x_ref[...] + jnp.dot(mix_ref[...], w_ref[...], preferred_element_type=F32)
    ms = jnp.mean(h * h, axis=-1, keepdims=True)
    o_ref[...] = h * lax.rsqrt(ms + RMS_EPS) * g_ref[...]


def _outproj(mix2d, x2d, w_bf16, g):
    rows = x2d.shape[0]
    tm = OUTPROJ_TM
    assert rows % tm == 0
    return pl.pallas_call(
        _outproj_body,
        grid=(rows // tm,),
        in_specs=[
            pl.BlockSpec((tm, D_MODEL), lambda i: (i, 0)),
            pl.BlockSpec((tm, D_MODEL), lambda i: (i, 0)),
            pl.BlockSpec((D_MODEL, D_MODEL), lambda i: (0, 0)),
            pl.BlockSpec((1, D_MODEL), lambda i: (0, 0)),
        ],
        out_specs=pl.BlockSpec((tm, D_MODEL), lambda i: (i, 0)),
        out_shape=jax.ShapeDtypeStruct((rows, D_MODEL), F32),
        compiler_params=pltpu.CompilerParams(
            dimension_semantics=("arbitrary",),
            vmem_limit_bytes=VMEM_LIMIT),
        name="outproj",
    )(mix2d, x2d, w_bf16, g)


def kernel(x, meta_tokens, norm_in_g, w_in, mu_shift, w0, w_lora_up, a0, a_lora_up, k_k, k_a, r_k, lnx_g, lnx_b, conv_w, w_out, norm_f_g):
    bsz, seq, _ = x.shape
    assert norm_in_g.shape[0] == 1 and seq % CHUNK == 0
    x2d = x.reshape(bsz * seq, D_MODEL)

    front = jnp.concatenate([jnp.zeros((FRONT_PAD, D_MODEL), x.dtype), meta_tokens.astype(x.dtype)], axis=0)
    g_in = norm_in_g[0][None, :]
    w_in_bf16 = w_in[0].astype(BF16)
    proj_meta = _inproj_meta(front, g_in, w_in_bf16)
    proj = _inproj(x, g_in, w_in_bf16)

    w_lora = jnp.zeros((2 * D_LORA, 2 * D_RWKV), F32)
    w_lora = w_lora.at[:D_LORA, :D_RWKV].set(w_lora_up[0]).at[D_LORA:, D_RWKV:].set(a_lora_up[0])
    vecs = jnp.concatenate([
        w0[0][None], a0[0][None], k_k[0][None], k_a[0][None], r_k[0].reshape(1, D_RWKV),
        lnx_g[0][None], lnx_b[0][None], conv_w[0],
        jnp.zeros((16 - 10, D_RWKV), F32)], axis=0)
    lane_head = jnp.arange(LANES) // HEAD
    seg_ones = (lane_head[:, None] == lane_head[None, :]).astype(BF16)
    t_idx = jnp.arange(CHUNK)
    tri = (t_idx[None, :] <= t_idx[:, None]).astype(BF16)
    tri2 = jnp.concatenate([tri, tri], axis=1)

    mix = _mixer(proj_meta, proj, mu_shift, vecs, w_lora.astype(BF16), seg_ones, tri2)

    out = _outproj(mix.reshape(bsz * seq, D_MODEL), x2d, w_out[0].astype(BF16), norm_f_g[None, :])
    return out.reshape(bsz, seq, D_MODEL)
```

```python
import math

import jax
import jax.numpy as jnp
from jax import lax
from jax.experimental import pallas as pl
from jax.experimental.pallas import tpu as pltpu

F32 = jnp.float32
BF16 = jnp.bfloat16

D_MODEL = 2048
N_META = 16
D_RWKV = 1024
HEAD = 64
N_HEADS = D_RWKV // HEAD
D_LORA = 64
D_CONV = 1024
RW_COLS = 4 * D_RWKV + 2 * D_LORA
IN_COLS = RW_COLS + 4 * D_CONV
LORA_COLS = 2 * D_LORA
MAIN_COLS = IN_COLS - LORA_COLS
RMS_EPS = 1e-6
LNX_EPS = 64e-5

CHUNK = 64
LANES = 128
SUBLANES = 8
N_PAIRS = D_RWKV // LANES
FRONT_PAD = CHUNK - N_META

VMEM_LIMIT = 56 * 1024 * 1024
INPROJ_VMEM_LIMIT = 60 * 1024 * 1024

INPROJ_TM = 1024
INPROJ_TN = 2048
META_TN = 1664
OUTPROJ_TM = 512


def _rmsnorm_bf16(x, g):
    ms = jnp.mean(x * x, axis=-1, keepdims=True)
    return (x * lax.rsqrt(ms + RMS_EPS) * g).astype(BF16)


def _inproj_body(x_ref, g_ref, w_ref, wl_ref, o_ref, ol_ref, hn_ref):
    @pl.when(pl.program_id(2) == 0)
    def _():
        hn_ref[...] = _rmsnorm_bf16(x_ref[0], g_ref[...])
        ol_ref[0] = jnp.dot(hn_ref[...], wl_ref[...], preferred_element_type=F32)

    o_ref[0] = jnp.dot(hn_ref[...], w_ref[...], preferred_element_type=F32)


def _main_col_start(j):
    skip = jnp.where(j * INPROJ_TN >= 4 * D_RWKV, LORA_COLS // LANES, 0)
    return (j * (INPROJ_TN // LANES) + skip) * LANES


def _inproj(x, g, w_bf16):
    bsz, seq, _ = x.shape
    tm, tn = INPROJ_TM, INPROJ_TN
    assert seq % tm == 0 and MAIN_COLS % tn == 0 and (4 * D_RWKV) % tn == 0
    return pl.pallas_call(
        _inproj_body,
        grid=(bsz, seq // tm, MAIN_COLS // tn),
        in_specs=[
            pl.BlockSpec((1, tm, D_MODEL), lambda b, i, j: (b, i, 0)),
            pl.BlockSpec((1, D_MODEL), lambda b, i, j: (0, 0)),
            pl.BlockSpec((pl.Element(D_MODEL), pl.Element(tn)), lambda b, i, j: (0, _main_col_start(j))),
            pl.BlockSpec((D_MODEL, LORA_COLS), lambda b, i, j: (0, 4 * D_RWKV // LORA_COLS)),
        ],
        out_specs=[pl.BlockSpec((1, tm, tn), lambda b, i, j: (b, i, j)),
                   pl.BlockSpec((1, tm, LORA_COLS), lambda b, i, j: (b, i, 0))],
        out_shape=[jax.ShapeDtypeStruct((bsz, seq, MAIN_COLS), F32),
                   jax.ShapeDtypeStruct((bsz, seq, LORA_COLS), F32)],
        scratch_shapes=[pltpu.VMEM((tm, D_MODEL), BF16)],
        compiler_params=pltpu.CompilerParams(
            dimension_semantics=("arbitrary", "arbitrary", "arbitrary"),
            vmem_limit_bytes=INPROJ_VMEM_LIMIT),
        name="inproj",
    )(x, g, w_bf16, w_bf16)


def _inproj_meta_body(x_ref, g_ref, w_ref, o_ref):
    o_ref[0] = jnp.dot(_rmsnorm_bf16(x_ref[...], g_ref[...]), w_ref[...], preferred_element_type=F32)


def _inproj_meta(front, g, w_bf16):
    tn = META_TN
    assert IN_COLS % tn == 0
    return pl.pallas_call(
        _inproj_meta_body,
        grid=(IN_COLS // tn,),
        in_specs=[
            pl.BlockSpec((CHUNK, D_MODEL), lambda j: (0, 0)),
            pl.BlockSpec((1, D_MODEL), lambda j: (0, 0)),
            pl.BlockSpec((D_MODEL, tn), lambda j: (0, j)),
        ],
        out_specs=pl.BlockSpec((1, CHUNK, tn), lambda j: (0, 0, j)),
        out_shape=jax.ShapeDtypeStruct((1, CHUNK, IN_COLS), F32),
        compiler_params=pltpu.CompilerParams(
            dimension_semantics=("arbitrary",),
            vmem_limit_bytes=VMEM_LIMIT),
        name="inproj_meta",
    )(front, g, w_bf16)


class _Split:
    def __init__(self, x):
        self.x = x
        self._hi = None
        self._lo = None

    @property
    def hi(self):
        if self._hi is None:
            self._hi = self.x.astype(BF16)
        return self._hi

    @property
    def lo(self):
        if self._lo is None:
            self._lo = (self.x - self.hi.astype(F32)).astype(BF16)
        return self._lo


class _BlockDiag:
    def __init__(self, y):
        self._y = y
        self._hi = None

    @property
    def hi(self):
        if self._hi is None:
            yp = self._y.hi
            lane = lax.broadcasted_iota(jnp.int32, yp.shape, 1)
            first = lane < HEAD
            zero = jnp.zeros_like(yp)
            self._hi = jnp.concatenate([jnp.where(first, yp, zero), jnp.where(first, zero, yp)], axis=0)
        return self._hi


def _mm(terms):
    la = jnp.concatenate([a.hi for a, _ in terms], axis=1)
    lb = jnp.concatenate([b.hi for _, b in terms], axis=0)
    return jnp.dot(la, lb, preferred_element_type=F32)


def _shift_rows(x, carry, n):
    row = lax.broadcasted_iota(jnp.int32, x.shape, 0)
    out = pltpu.roll(x, n, axis=0)
    for i in range(n):
        src = SUBLANES - n + i
        out = jnp.where(row == i, carry[src:src + 1, :], out)
    return out


def _sigmoid(x):
    return 0.5 + 0.5 * jnp.tanh(0.5 * x)


def _silu(x):
    h = 0.5 * x
    return h + h * jnp.tanh(h)


def _interleave(order, stages):
    for name in order:
        next(stages[name], None)
    for gen in stages.values():
        for _ in gen:
            pass


_HALF_ORDER = ("advance", "inverse", "prep", "advance", "inverse", "advance", "inverse", "finish", "advance",
               "prep", "finish", "inverse", "inverse", "prep", "finish", "inverse", "inverse", "prep",
               "inverse", "inverse", "inverse", "inverse")


def _mixer_body(pm_ref, lm_ref, px_ref, lx_ref, mu_ref, vec_ref, wl_ref, ones_ref, tri_ref, o_ref,
                carry_rw, carry_u, state, s_at, s_rt, s_kh, s_bh, s_v, s_kd, s_bd, s_pe, s_y,
                r_x, r_tak, r_ark, r_arb, e_bonus, e_gate, e_yb):
    def vec(i):
        return vec_ref[i:i + 1, :]

    w0, a0, k_k, k_a, r_k, lnx_g, lnx_b, cw0, cw1, cw2 = (vec(i) for i in range(10))
    ps = range(N_PAIRS)

    def seg_sums(xs):
        tiles = [x[:, p * LANES:(p + 1) * LANES] for x in xs for p in ps]
        t = jnp.concatenate(tiles, axis=0).astype(BF16)
        s = jnp.dot(t, ones_ref[...], preferred_element_type=F32)
        out = []
        for i in range(len(xs)):
            rows = [s[(i * N_PAIRS + p) * CHUNK:(i * N_PAIRS + p + 1) * CHUNK] for p in ps]
            out.append(jnp.concatenate(rows, axis=1))
        return out

    def prep(p_ref, l_ref, row0, sel):
        p_rw = jnp.concatenate([p_ref[0, row0:row0 + CHUNK, :4 * D_RWKV], l_ref[0, row0:row0 + CHUNK, :]],
                               axis=1)
        prev = _shift_rows(p_rw, carry_rw[...], 1)
        carry_rw[...] = p_rw[CHUNK - SUBLANES:, :]
        pm = p_rw + (prev - p_rw) * mu_ref[...]
        r = pm[:, 0 * D_RWKV:1 * D_RWKV]
        k = pm[:, 1 * D_RWKV:2 * D_RWKV]
        v = pm[:, 2 * D_RWKV:3 * D_RWKV]
        g_r = pm[:, 3 * D_RWKV:4 * D_RWKV]
        lora_in = pm[:, 4 * D_RWKV:]
        lane = lax.broadcasted_iota(jnp.int32, lora_in.shape, 1)
        lora_in = jnp.where(lane < D_LORA, jnp.tanh(lora_in), lora_in).astype(BF16)
        yield
        lora = jnp.dot(lora_in, wl_ref[...], preferred_element_type=F32)
        logw = (-math.exp(-0.5)) * _sigmoid(w0 + lora[:, :D_RWKV])
        a = _sigmoid(a0 + lora[:, D_RWKV:])
        yield
        logw = _Split(logw)
        cum = jnp.dot(tri_ref[...], jnp.concatenate([logw.hi, logw.lo], axis=0),
                      preferred_element_type=F32)
        logw = logw.x
        kk = k * k_k
        k = k * (1.0 + (a - 1.0) * k_a)
        yield
        kk_sq, rk_sum = seg_sums([kk * kk, r * k * r_k])
        kk = kk * lax.rsqrt(jnp.maximum(kk_sq, 1e-24))
        beta = kk * a
        e_in = jnp.exp(cum)
        e_out = jnp.exp(-cum)
        e_ex = jnp.exp(cum - logw)
        p_end = e_in[CHUNK - 1:CHUNK, :]
        a_t = kk * e_ex
        r_t = r * e_in
        k_h = k * e_out
        b_h = beta * e_out
        k_d = k_h * p_end
        b_d = b_h * p_end
        for dst, val in ((s_at, a_t), (s_rt, r_t), (s_kh, k_h), (s_bh, b_h), (s_v, v), (s_kd, k_d), (s_bd, b_d)):
            val = val.astype(dst.dtype)
            for p in ps:
                dst[sel, p] = val[:, p * LANES:(p + 1) * LANES]
        for p in ps:
            s_pe[sel, p] = jnp.broadcast_to(p_end[:, p * LANES:(p + 1) * LANES], (SUBLANES, LANES))
        e_bonus[sel] = rk_sum * v
        e_gate[sel] = _silu(g_r)

        p_cv = p_ref[0, row0:row0 + CHUNK, 4 * D_RWKV:]
        b_g = p_cv[:, 0 * D_CONV:1 * D_CONV]
        c_g = p_cv[:, 1 * D_CONV:2 * D_CONV]
        h_c = p_cv[:, 2 * D_CONV:3 * D_CONV]
        g_c = p_cv[:, 3 * D_CONV:4 * D_CONV]
        u_c = c_g * h_c
        carry = carry_u[...]
        conv = cw0 * _shift_rows(u_c, carry, 2) + cw1 * _shift_rows(u_c, carry, 1) + cw2 * u_c
        carry_u[...] = u_c[CHUNK - SUBLANES:, :]
        e_yb[sel] = (b_g * conv * _silu(g_c)).astype(e_yb.dtype)

    t_i = lax.broadcasted_iota(jnp.int32, (CHUNK, LANES), 0)
    j_i = lax.broadcasted_iota(jnp.int32, (CHUNK, LANES), 1) & (HEAD - 1)

    def splits(xs):
        return [_Split(x) for x in xs]

    def pmm(xs, ys):
        return [_mm([(x, _BlockDiag(y))]) for x, y in zip(xs, ys)]

    def inverse(sel):
        strict = j_i < t_i
        incl = j_i <= t_i
        eye = (j_i == t_i).astype(F32)

        def same_block(log2_size):
            return (j_i >> log2_size) == (t_i >> log2_size)

        g = []
        for p in ps:
            lhs = jnp.concatenate([s_at[sel, p], s_rt[sel, p]], axis=0)
            rhs_t = jnp.concatenate([_BlockDiag(_Split(s_bh[sel, p])).hi,
                                     _BlockDiag(_Split(s_kh[sel, p])).hi], axis=0)
            g.append(lax.dot_general(lhs, rhs_t, (((1,), (1,)), ((), ())), preferred_element_type=F32))
        t_ab = [jnp.where(strict, g_[:CHUNK, :LANES], 0.0) for g_ in g]
        for p in ps:
            r_arb[sel, p] = jnp.where(incl, g[p][CHUNK:, :LANES], 0.0).astype(r_arb.dtype)
            r_tak[sel, p] = jnp.where(strict, g[p][:CHUNK, LANES:], 0.0).astype(r_tak.dtype)
            r_ark[sel, p] = jnp.where(incl, g[p][CHUNK:, LANES:], 0.0).astype(r_ark.dtype)
        yield

        x = splits([eye - jnp.where(same_block(1), t, 0.0) for t in t_ab])
        for log2_size in range(2, 7):
            level = same_block(log2_size) & jnp.logical_not(same_block(log2_size - 1))
            c = splits([jnp.where(level, t, 0.0) for t in t_ab])
            xc = splits(pmm(x, c))
            yield
            x = splits([x_.x - m for x_, m in zip(x, pmm(xc, x))])
            if log2_size < 6:
                yield
        for p in ps:
            r_x[sel, p] = x[p].hi

    def advance(sel):
        row = lax.broadcasted_iota(jnp.int32, (LANES, LANES), 0)
        col = lax.broadcasted_iota(jnp.int32, (LANES, LANES), 1)
        same_head = (row < HEAD) == (col < HEAD)
        vv = [s_v[sel, p] for p in ps]
        vbd = [_BlockDiag(_Split(v_)) for v_ in vv]
        s0 = [state[p] for p in ps]
        h0 = splits([s.T for s in s0])
        rhs = splits([_mm([(_Split(s_at[sel, p]), h0[p]), (_Split(r_tak[sel, p]), vbd[p])]) for p in ps])
        yield
        u = splits(pmm(splits([r_x[sel, p] for p in ps]), rhs))
        yield
        for p in ps:
            s_y[sel, p] = _mm([(_Split(s_rt[sel, p]), h0[p]), (_Split(r_ark[sel, p]), vbd[p]),
                               (_Split(-r_arb[sel, p]), _BlockDiag(u[p]))])
        yield
        for p in ps:
            wv_t = _Split(jnp.concatenate([vv[p], u[p].x], axis=0).T)
            upd = _mm([(wv_t, _Split(jnp.concatenate([s_kd[sel, p], -s_bd[sel, p]], axis=0)))])
            state[p] = jnp.where(same_head, s0[p] * s_pe[sel, p][0:1, :] + upd, 0.0)

    def finish(sel, row0):
        y = jnp.concatenate([s_y[sel, p] for p in ps], axis=1)
        inv_n = 1.0 / HEAD
        yield
        mean = seg_sums([y])[0] * inv_n
        yc = y - mean
        yield
        var = seg_sums([yc * yc])[0] * inv_n
        yn = yc * lax.rsqrt(var + LNX_EPS) * lnx_g + lnx_b
        y_a = (yn + e_bonus[sel]) * e_gate[sel]
        o_ref[0, row0:row0 + CHUNK, :D_RWKV] = y_a.astype(o_ref.dtype)
        o_ref[0, row0:row0 + CHUNK, D_RWKV:] = e_yb[sel]

    @pl.when(pl.program_id(1) == 0)
    def _():
        for ref in (carry_rw, carry_u, state, s_at, s_rt, s_v, s_kd, s_bd, s_pe,
                    r_x, r_tak, r_ark, r_arb, e_bonus, e_gate, e_yb):
            ref[...] = jnp.zeros_like(ref)
        for _ in prep(pm_ref, lm_ref, 0, 0):
            pass

    _interleave(_HALF_ORDER, {"advance": advance(1), "finish": finish(1, 0), "inverse": inverse(0),
                              "prep": prep(px_ref, lx_ref, 0, 1)})
    _interleave(_HALF_ORDER, {"advance": advance(0), "finish": finish(0, CHUNK), "inverse": inverse(1),
                              "prep": prep(px_ref, lx_ref, CHUNK, 0)})


def _mixer(proj_meta, lora_meta, proj, lora, mu, vecs, w_lora, seg_ones, tri2):
    bsz, seq, _ = proj.shape
    assert seq % (2 * CHUNK) == 0
    n_steps = seq // (2 * CHUNK) + 1
    pair_buf = pltpu.VMEM((2, N_PAIRS, CHUNK, LANES), F32)
    pair_bf16 = pltpu.VMEM((2, N_PAIRS, CHUNK, LANES), BF16)
    row_buf = pltpu.VMEM((2, CHUNK, D_RWKV), F32)
    const = lambda shape: pl.BlockSpec(shape, lambda b, j: (0,) * len(shape))
    return pl.pallas_call(
        _mixer_body,
        grid=(bsz, n_steps),
        in_specs=[
            const(proj_meta.shape), const(lora_meta.shape),
            pl.BlockSpec((1, 2 * CHUNK, MAIN_COLS), lambda b, j: (b, jnp.minimum(j, n_steps - 2), 0)),
            pl.BlockSpec((1, 2 * CHUNK, LORA_COLS), lambda b, j: (b, jnp.minimum(j, n_steps - 2), 0)),
            const(mu.shape), const(vecs.shape), const(w_lora.shape),
            const(seg_ones.shape), const(tri2.shape),
        ],
        out_specs=pl.BlockSpec((1, 2 * CHUNK, D_MODEL), lambda b, j: (b, jnp.maximum(j - 1, 0), 0)),
        out_shape=jax.ShapeDtypeStruct((bsz, seq, D_MODEL), BF16),
        scratch_shapes=[
            pltpu.VMEM((SUBLANES, RW_COLS), F32),
            pltpu.VMEM((SUBLANES, D_CONV), F32),
            pltpu.VMEM((N_PAIRS, LANES, LANES), F32),
            pair_bf16, pair_bf16, pair_bf16, pair_bf16, pair_buf, pair_bf16, pair_bf16,
            pltpu.VMEM((2, N_PAIRS, SUBLANES, LANES), F32),
            pair_buf,
            pair_bf16, pair_bf16, pair_bf16, pair_bf16,
            row_buf, row_buf,
            pltpu.VMEM((2, CHUNK, D_CONV), BF16),
        ],
        compiler_params=pltpu.CompilerParams(
            dimension_semantics=("arbitrary", "arbitrary"),
            vmem_limit_bytes=VMEM_LIMIT),
        name="mixer",
    )(proj_meta, lora_meta, proj, lora, mu, vecs, w_lora, seg_ones, tri2)


def _outproj_body(mix_ref, x_ref, w_ref, g_ref, o_ref):
    h = x_ref[...] + jnp.dot(mix_ref[...], w_ref[...], preferred_element_type=F32)
    ms = jnp.mean(h * h, axis=-1, keepdims=True)
    o_ref[...] = h * lax.rsqrt(ms + RMS_EPS) * g_ref[...]


def _outproj(mix2d, x2d, w_bf16, g):
    rows = x2d.shape[0]
    tm = OUTPROJ_TM
    assert rows % tm == 0
    return pl.pallas_call(
        _outproj_body,
        grid=(rows // tm,),
        in_specs=[
            pl.BlockSpec((tm, D_MODEL), lambda i: (i, 0)),
            pl.BlockSpec((tm, D_MODEL), lambda i: (i, 0)),
            pl.BlockSpec((D_MODEL, D_MODEL), lambda i: (0, 0)),
            pl.BlockSpec((1, D_MODEL), lambda i: (0, 0)),
        ],
        out_specs=pl.BlockSpec((tm, D_MODEL), lambda i: (i, 0)),
        out_shape=jax.ShapeDtypeStruct((rows, D_MODEL), F32),
        compiler_params=pltpu.CompilerParams(
            dimension_semantics=("arbitrary",),
            vmem_limit_bytes=VMEM_LIMIT),
        name="outproj",
    )(mix2d, x2d, w_bf16, g)


def kernel(x, meta_tokens, norm_in_g, w_in, mu_shift, w0, w_lora_up, a0, a_lora_up, k_k, k_a, r_k, lnx_g, lnx_b, conv_w, w_out, norm_f_g):
    bsz, seq, _ = x.shape
    assert norm_in_g.shape[0] == 1 and seq % CHUNK == 0
    x2d = x.reshape(bsz * seq, D_MODEL)

    front = jnp.concatenate([jnp.zeros((FRONT_PAD, D_MODEL), x.dtype), meta_tokens.astype(x.dtype)], axis=0)
    g_in = norm_in_g[0][None, :]
    w_in_bf16 = w_in[0].astype(BF16)
    meta_all = _inproj_meta(front, g_in, w_in_bf16)
    proj_meta = jnp.concatenate([meta_all[..., :4 * D_RWKV], meta_all[..., RW_COLS:]], axis=-1)
    lora_meta = meta_all[..., 4 * D_RWKV:RW_COLS]
    proj, lora = _inproj(x, g_in, w_in_bf16)

    w_lora = jnp.zeros((2 * D_LORA, 2 * D_RWKV), F32)
    w_lora = w_lora.at[:D_LORA, :D_RWKV].set(w_lora_up[0]).at[D_LORA:, D_RWKV:].set(a_lora_up[0])
    vecs = jnp.concatenate([
        w0[0][None], a0[0][None], k_k[0][None], k_a[0][None], r_k[0].reshape(1, D_RWKV),
        lnx_g[0][None], lnx_b[0][None], conv_w[0],
        jnp.zeros((16 - 10, D_RWKV), F32)], axis=0)
    lane_head = jnp.arange(LANES) // HEAD
    seg_ones = (lane_head[:, None] == lane_head[None, :]).astype(BF16)
    t_idx = jnp.arange(CHUNK)
    tri = (t_idx[None, :] <= t_idx[:, None]).astype(BF16)
    tri2 = jnp.concatenate([tri, tri], axis=1)

    mix = _mixer(proj_meta, lora_meta, proj, lora, mu_shift, vecs, w_lora.astype(BF16), seg_ones, tri2)

    out = _outproj(mix.reshape(bsz * seq, D_MODEL), x2d, w_out[0].astype(BF16), norm_f_g[None, :])
    return out.reshape(bsz, seq, D_MODEL)
```

```python
import math

import jax
import jax.numpy as jnp
from jax import lax
from jax.experimental import pallas as pl
from jax.experimental.pallas import tpu as pltpu

F32 = jnp.float32
BF16 = jnp.bfloat16

D_MODEL = 2048
N_META = 16
D_RWKV = 1024
HEAD = 64
N_HEADS = D_RWKV // HEAD
D_LORA = 64
D_CONV = 1024
RW_COLS = 4 * D_RWKV + 2 * D_LORA
IN_COLS = RW_COLS + 4 * D_CONV
LORA_COLS = 2 * D_LORA
MAIN_COLS = IN_COLS - LORA_COLS
RMS_EPS = 1e-6
LNX_EPS = 64e-5

CHUNK = 64
LANES = 128
SUBLANES = 8
N_PAIRS = D_RWKV // LANES
FRONT_PAD = CHUNK - N_META

VMEM_LIMIT = 56 * 1024 * 1024
INPROJ_VMEM_LIMIT = 60 * 1024 * 1024

INPROJ_TM = 1024
INPROJ_TN = 2048
META_TN = 1664
OUTPROJ_TM = 512


def _rmsnorm_bf16(x, g):
    ms = jnp.mean(x * x, axis=-1, keepdims=True)
    return (x * lax.rsqrt(ms + RMS_EPS) * g).astype(BF16)


def _inproj_body(x_ref, g_ref, w_ref, wl_ref, o_ref, ol_ref, hn_ref):
    @pl.when(pl.program_id(2) == 0)
    def _():
        hn_ref[...] = _rmsnorm_bf16(x_ref[0], g_ref[...])
        ol_ref[0] = jnp.dot(hn_ref[...], wl_ref[...], preferred_element_type=F32)

    o_ref[0] = jnp.dot(hn_ref[...], w_ref[...], preferred_element_type=F32)


def _main_col_start(j):
    skip = jnp.where(j * INPROJ_TN >= 4 * D_RWKV, LORA_COLS // LANES, 0)
    return (j * (INPROJ_TN // LANES) + skip) * LANES


def _inproj(x, g, w_bf16):
    bsz, seq, _ = x.shape
    tm, tn = INPROJ_TM, INPROJ_TN
    assert seq % tm == 0 and MAIN_COLS % tn == 0 and (4 * D_RWKV) % tn == 0
    return pl.pallas_call(
        _inproj_body,
        grid=(bsz, seq // tm, MAIN_COLS // tn),
        in_specs=[
            pl.BlockSpec((1, tm, D_MODEL), lambda b, i, j: (b, i, 0)),
            pl.BlockSpec((1, D_MODEL), lambda b, i, j: (0, 0)),
            pl.BlockSpec((pl.Element(D_MODEL), pl.Element(tn)), lambda b, i, j: (0, _main_col_start(j))),
            pl.BlockSpec((D_MODEL, LORA_COLS), lambda b, i, j: (0, 4 * D_RWKV // LORA_COLS)),
        ],
        out_specs=[pl.BlockSpec((1, tm, tn), lambda b, i, j: (b, i, j)),
                   pl.BlockSpec((1, tm, LORA_COLS), lambda b, i, j: (b, i, 0))],
        out_shape=[jax.ShapeDtypeStruct((bsz, seq, MAIN_COLS), F32),
                   jax.ShapeDtypeStruct((bsz, seq, LORA_COLS), F32)],
        scratch_shapes=[pltpu.VMEM((tm, D_MODEL), BF16)],
        compiler_params=pltpu.CompilerParams(
            dimension_semantics=("arbitrary", "arbitrary", "arbitrary"),
            vmem_limit_bytes=INPROJ_VMEM_LIMIT),
        name="inproj",
    )(x, g, w_bf16, w_bf16)


def _inproj_meta_body(x_ref, g_ref, w_ref, o_ref):
    o_ref[0] = jnp.dot(_rmsnorm_bf16(x_ref[...], g_ref[...]), w_ref[...], preferred_element_type=F32)


def _inproj_meta(front, g, w_bf16):
    tn = META_TN
    assert IN_COLS % tn == 0
    return pl.pallas_call(
        _inproj_meta_body,
        grid=(IN_COLS // tn,),
        in_specs=[
            pl.BlockSpec((CHUNK, D_MODEL), lambda j: (0, 0)),
            pl.BlockSpec((1, D_MODEL), lambda j: (0, 0)),
            pl.BlockSpec((D_MODEL, tn), lambda j: (0, j)),
        ],
        out_specs=pl.BlockSpec((1, CHUNK, tn), lambda j: (0, 0, j)),
        out_shape=jax.ShapeDtypeStruct((1, CHUNK, IN_COLS), F32),
        compiler_params=pltpu.CompilerParams(
            dimension_semantics=("arbitrary",),
            vmem_limit_bytes=VMEM_LIMIT),
        name="inproj_meta",
    )(front, g, w_bf16)


class _Split:
    def __init__(self, x):
        self.x = x
        self._hi = None
        self._lo = None

    @property
    def hi(self):
        if self._hi is None:
            self._hi = self.x.astype(BF16)
        return self._hi

    @property
    def lo(self):
        if self._lo is None:
            self._lo = (self.x - self.hi.astype(F32)).astype(BF16)
        return self._lo


class _BlockDiag:
    def __init__(self, y):
        self._y = y
        self._hi = None

    @property
    def hi(self):
        if self._hi is None:
            yp = self._y.hi
            lane = lax.broadcasted_iota(jnp.int32, yp.shape, 1)
            first = lane < HEAD
            zero = jnp.zeros_like(yp)
            self._hi = jnp.concatenate([jnp.where(first, yp, zero), jnp.where(first, zero, yp)], axis=0)
        return self._hi


def _mm(terms):
    la = jnp.concatenate([a.hi for a, _ in terms], axis=1)
    lb = jnp.concatenate([b.hi for _, b in terms], axis=0)
    return jnp.dot(la, lb, preferred_element_type=F32)


def _shift_rows(x, carry, n):
    row = lax.broadcasted_iota(jnp.int32, x.shape, 0)
    out = pltpu.roll(x, n, axis=0)
    for i in range(n):
        src = SUBLANES - n + i
        out = jnp.where(row == i, carry[src:src + 1, :], out)
    return out


def _sigmoid(x):
    return 0.5 + 0.5 * jnp.tanh(0.5 * x)


def _silu(x):
    h = 0.5 * x
    return h + h * jnp.tanh(h)


def _interleave(order, stages):
    for name in order:
        next(stages[name], None)
    for gen in stages.values():
        for _ in gen:
            pass


_HALF_ORDER = ("inverse", "inverse", "advance", "inverse", "advance", "inverse", "advance", "finish", "inverse",
               "advance", "finish", "inverse", "prep", "finish", "inverse", "prep", "inverse", "inverse", "prep",
               "inverse", "prep", "inverse")


def _mixer_body(pm_ref, lm_ref, px_ref, lx_ref, mu_ref, vec_ref, wl_ref, ones_ref, tri_ref, o_ref,
                carry_rw, carry_u, state, s_at, s_rt, s_kh, s_bh, s_v, s_pe, s_y,
                r_x, r_tak, r_ark, r_arb, e_bonus, e_gate, e_yb):
    def vec(i):
        return vec_ref[i:i + 1, :]

    w0, a0, k_k, k_a, r_k, lnx_g, lnx_b, cw0, cw1, cw2 = (vec(i) for i in range(10))
    ps = range(N_PAIRS)

    def seg_sums(xs):
        tiles = [x[:, p * LANES:(p + 1) * LANES] for x in xs for p in ps]
        t = jnp.concatenate(tiles, axis=0).astype(BF16)
        s = jnp.dot(t, ones_ref[...], preferred_element_type=F32)
        out = []
        for i in range(len(xs)):
            rows = [s[(i * N_PAIRS + p) * CHUNK:(i * N_PAIRS + p + 1) * CHUNK] for p in ps]
            out.append(jnp.concatenate(rows, axis=1))
        return out

    def prep(p_ref, l_ref, row0, sel):
        p_rw = jnp.concatenate([p_ref[0, row0:row0 + CHUNK, :4 * D_RWKV], l_ref[0, row0:row0 + CHUNK, :]],
                               axis=1)
        prev = _shift_rows(p_rw, carry_rw[...], 1)
        carry_rw[...] = p_rw[CHUNK - SUBLANES:, :]
        pm = p_rw + (prev - p_rw) * mu_ref[...]
        r = pm[:, 0 * D_RWKV:1 * D_RWKV]
        k = pm[:, 1 * D_RWKV:2 * D_RWKV]
        v = pm[:, 2 * D_RWKV:3 * D_RWKV]
        g_r = pm[:, 3 * D_RWKV:4 * D_RWKV]
        lora_in = pm[:, 4 * D_RWKV:]
        lane = lax.broadcasted_iota(jnp.int32, lora_in.shape, 1)
        lora_in = jnp.where(lane < D_LORA, jnp.tanh(lora_in), lora_in).astype(BF16)
        yield
        lora = jnp.dot(lora_in, wl_ref[...], preferred_element_type=F32)
        logw = (-math.exp(-0.5)) * _sigmoid(w0 + lora[:, :D_RWKV])
        a = _sigmoid(a0 + lora[:, D_RWKV:])
        yield
        logw = _Split(logw)
        cum = jnp.dot(tri_ref[...], jnp.concatenate([logw.hi, logw.lo], axis=0),
                      preferred_element_type=F32)
        logw = logw.x
        kk = k * k_k
        k = k * (1.0 + (a - 1.0) * k_a)
        yield
        kk_sq, rk_sum = seg_sums([kk * kk, r * k * r_k])
        kk = kk * lax.rsqrt(jnp.maximum(kk_sq, 1e-24))
        beta = kk * a
        e_in = jnp.exp(cum)
        e_out = jnp.exp(-cum)
        e_ex = jnp.exp(cum - logw)
        p_end = e_in[CHUNK - 1:CHUNK, :]
        a_t = kk * e_ex
        r_t = r * e_in
        k_h = k * e_out
        b_h = beta * e_out
        for dst, val in ((s_at, a_t), (s_rt, r_t), (s_kh, k_h), (s_bh, b_h), (s_v, v)):
            val = val.astype(dst.dtype)
            for p in ps:
                dst[sel, p] = val[:, p * LANES:(p + 1) * LANES]
        for p in ps:
            s_pe[sel, p] = jnp.broadcast_to(p_end[:, p * LANES:(p + 1) * LANES], (SUBLANES, LANES))
        e_bonus[sel] = rk_sum * v
        e_gate[sel] = _silu(g_r)

        p_cv = p_ref[0, row0:row0 + CHUNK, 4 * D_RWKV:]
        b_g = p_cv[:, 0 * D_CONV:1 * D_CONV]
        c_g = p_cv[:, 1 * D_CONV:2 * D_CONV]
        h_c = p_cv[:, 2 * D_CONV:3 * D_CONV]
        g_c = p_cv[:, 3 * D_CONV:4 * D_CONV]
        u_c = c_g * h_c
        carry = carry_u[...]
        conv = cw0 * _shift_rows(u_c, carry, 2) + cw1 * _shift_rows(u_c, carry, 1) + cw2 * u_c
        carry_u[...] = u_c[CHUNK - SUBLANES:, :]
        e_yb[sel] = (b_g * conv * _silu(g_c)).astype(e_yb.dtype)

    t_i = lax.broadcasted_iota(jnp.int32, (CHUNK, LANES), 0)
    j_i = lax.broadcasted_iota(jnp.int32, (CHUNK, LANES), 1) & (HEAD - 1)

    def splits(xs):
        return [_Split(x) for x in xs]

    def pmm(xs, ys):
        return [_mm([(x, _BlockDiag(y))]) for x, y in zip(xs, ys)]

    def inverse(sel):
        strict = j_i < t_i
        incl = j_i <= t_i
        eye = (j_i == t_i).astype(F32)

        def same_block(log2_size):
            return (j_i >> log2_size) == (t_i >> log2_size)

        g = []
        for p in ps:
            lhs = jnp.concatenate([s_at[sel, p], s_rt[sel, p]], axis=0)
            rhs_t = jnp.concatenate([_BlockDiag(_Split(s_bh[sel, p])).hi,
                                     _BlockDiag(_Split(s_kh[sel, p])).hi], axis=0)
            g.append(lax.dot_general(lhs, rhs_t, (((1,), (1,)), ((), ())), preferred_element_type=F32))
        t_ab = [jnp.where(strict, g_[:CHUNK, :LANES], 0.0) for g_ in g]
        for p in ps:
            r_arb[sel, p] = jnp.where(incl, g[p][CHUNK:, :LANES], 0.0).astype(r_arb.dtype)
            r_tak[sel, p] = jnp.where(strict, g[p][:CHUNK, LANES:], 0.0).astype(r_tak.dtype)
            r_ark[sel, p] = jnp.where(incl, g[p][CHUNK:, LANES:], 0.0).astype(r_ark.dtype)
        yield

        x = splits([eye - jnp.where(same_block(1), t, 0.0) for t in t_ab])
        for log2_size in range(2, 7):
            level = same_block(log2_size) & jnp.logical_not(same_block(log2_size - 1))
            c = splits([jnp.where(level, t, 0.0) for t in t_ab])
            xc = splits(pmm(x, c))
            yield
            x = splits([x_.x - m for x_, m in zip(x, pmm(xc, x))])
            if log2_size < 6:
                yield
        for p in ps:
            r_x[sel, p] = x[p].hi

    def advance(sel):
        row = lax.broadcasted_iota(jnp.int32, (LANES, LANES), 0)
        col = lax.broadcasted_iota(jnp.int32, (LANES, LANES), 1)
        same_head = (row < HEAD) == (col < HEAD)
        vv = [s_v[sel, p] for p in ps]
        vbd = [_BlockDiag(_Split(v_)) for v_ in vv]
        s0 = [state[p] for p in ps]
        h0 = splits([s.T for s in s0])
        rhs = splits([_mm([(_Split(s_at[sel, p]), h0[p]), (_Split(r_tak[sel, p]), vbd[p])]) for p in ps])
        yield
        u = splits(pmm(splits([r_x[sel, p] for p in ps]), rhs))
        yield
        for p in ps:
            s_y[sel, p] = _mm([(_Split(s_rt[sel, p]), h0[p]), (_Split(r_ark[sel, p]), vbd[p]),
                               (_Split(-r_arb[sel, p]), _BlockDiag(u[p]))])
        yield
        for p in ps:
            wv_t = _Split(jnp.concatenate([vv[p], u[p].x], axis=0).T)
            upd = _mm([(wv_t, _Split(jnp.concatenate([s_kh[sel, p], -s_bh[sel, p]], axis=0)))])
            state[p] = jnp.where(same_head, (s0[p] + upd) * s_pe[sel, p][0:1, :], 0.0)

    def finish(sel, row0):
        y = jnp.concatenate([s_y[sel, p] for p in ps], axis=1)
        inv_n = 1.0 / HEAD
        yield
        mean = seg_sums([y])[0] * inv_n
        yc = y - mean
        yield
        var = seg_sums([yc * yc])[0] * inv_n
        yn = yc * lax.rsqrt(var + LNX_EPS) * lnx_g + lnx_b
        y_a = (yn + e_bonus[sel]) * e_gate[sel]
        o_ref[0, row0:row0 + CHUNK, :D_RWKV] = y_a.astype(o_ref.dtype)
        o_ref[0, row0:row0 + CHUNK, D_RWKV:] = e_yb[sel]

    @pl.when(pl.program_id(1) == 0)
    def _():
        for ref in (carry_rw, carry_u, state, s_at, s_rt, s_kh, s_bh, s_v, s_pe,
                    r_x, r_tak, r_ark, r_arb, e_bonus, e_gate, e_yb):
            ref[...] = jnp.zeros_like(ref)
        for _ in prep(pm_ref, lm_ref, 0, 0):
            pass

    _interleave(_HALF_ORDER, {"advance": advance(1), "finish": finish(1, 0), "inverse": inverse(0),
                              "prep": prep(px_ref, lx_ref, 0, 1)})
    _interleave(_HALF_ORDER, {"advance": advance(0), "finish": finish(0, CHUNK), "inverse": inverse(1),
                              "prep": prep(px_ref, lx_ref, CHUNK, 0)})


def _mixer(proj_meta, lora_meta, proj, lora, mu, vecs, w_lora, seg_ones, tri2):
    bsz, seq, _ = proj.shape
    assert seq % (2 * CHUNK) == 0
    n_steps = seq // (2 * CHUNK) + 1
    pair_buf = pltpu.VMEM((2, N_PAIRS, CHUNK, LANES), F32)
    pair_bf16 = pltpu.VMEM((2, N_PAIRS, CHUNK, LANES), BF16)
    row_buf = pltpu.VMEM((2, CHUNK, D_RWKV), F32)
    const = lambda shape: pl.BlockSpec(shape, lambda b, j: (0,) * len(shape))
    return pl.pallas_call(
        _mixer_body,
        grid=(bsz, n_steps),
        in_specs=[
            const(proj_meta.shape), const(lora_meta.shape),
            pl.BlockSpec((1, 2 * CHUNK, MAIN_COLS), lambda b, j: (b, jnp.minimum(j, n_steps - 2), 0)),
            pl.BlockSpec((1, 2 * CHUNK, LORA_COLS), lambda b, j: (b, jnp.minimum(j, n_steps - 2), 0)),
            const(mu.shape), const(vecs.shape), const(w_lora.shape),
            const(seg_ones.shape), const(tri2.shape),
        ],
        out_specs=pl.BlockSpec((1, 2 * CHUNK, D_MODEL), lambda b, j: (b, jnp.maximum(j - 1, 0), 0)),
        out_shape=jax.ShapeDtypeStruct((bsz, seq, D_MODEL), BF16),
        scratch_shapes=[
            pltpu.VMEM((SUBLANES, RW_COLS), F32),
            pltpu.VMEM((SUBLANES, D_CONV), F32),
            pltpu.VMEM((N_PAIRS, LANES, LANES), F32),
            pair_bf16, pair_bf16, pair_bf16, pair_bf16, pair_buf,
            pltpu.VMEM((2, N_PAIRS, SUBLANES, LANES), F32),
            pair_buf,
            pair_bf16, pair_bf16, pair_bf16, pair_bf16,
            row_buf, row_buf,
            pltpu.VMEM((2, CHUNK, D_CONV), BF16),
        ],
        compiler_params=pltpu.CompilerParams(
            dimension_semantics=("arbitrary", "arbitrary"),
            vmem_limit_bytes=VMEM_LIMIT),
        name="mixer",
    )(proj_meta, lora_meta, proj, lora, mu, vecs, w_lora, seg_ones, tri2)


def _outproj_body(mix_ref, x_ref, w_ref, g_ref, o_ref):
    h = x_ref[...] + jnp.dot(mix_ref[...], w_ref[...], preferred_element_type=F32)
    ms = jnp.mean(h * h, axis=-1, keepdims=True)
    o_ref[...] = h * lax.rsqrt(ms + RMS_EPS) * g_ref[...]


def _outproj(mix2d, x2d, w_bf16, g):
    rows = x2d.shape[0]
    tm = OUTPROJ_TM
    assert rows % tm == 0
    return pl.pallas_call(
        _outproj_body,
        grid=(rows // tm,),
        in_specs=[
            pl.BlockSpec((tm, D_MODEL), lambda i: (i, 0)),
            pl.BlockSpec((tm, D_MODEL), lambda i: (i, 0)),
            pl.BlockSpec((D_MODEL, D_MODEL), lambda i: (0, 0)),
            pl.BlockSpec((1, D_MODEL), lambda i: (0, 0)),
        ],
        out_specs=pl.BlockSpec((tm, D_MODEL), lambda i: (i, 0)),
        out_shape=jax.ShapeDtypeStruct((rows, D_MODEL), F32),
        compiler_params=pltpu.CompilerParams(
            dimension_semantics=("arbitrary",),
            vmem_limit_bytes=VMEM_LIMIT),
        name="outproj",
    )(mix2d, x2d, w_bf16, g)


def kernel(x, meta_tokens, norm_in_g, w_in, mu_shift, w0, w_lora_up, a0, a_lora_up, k_k, k_a, r_k, lnx_g, lnx_b, conv_w, w_out, norm_f_g):
    bsz, seq, _ = x.shape
    assert norm_in_g.shape[0] == 1 and seq % CHUNK == 0
    x2d = x.reshape(bsz * seq, D_MODEL)

    front = jnp.concatenate([jnp.zeros((FRONT_PAD, D_MODEL), x.dtype), meta_tokens.astype(x.dtype)], axis=0)
    g_in = norm_in_g[0][None, :]
    w_in_bf16 = w_in[0].astype(BF16)
    meta_all = _inproj_meta(front, g_in, w_in_bf16)
    proj_meta = jnp.concatenate([meta_all[..., :4 * D_RWKV], meta_all[..., RW_COLS:]], axis=-1)
    lora_meta = meta_all[..., 4 * D_RWKV:RW_COLS]
    proj, lora = _inproj(x, g_in, w_in_bf16)

    w_lora = jnp.zeros((2 * D_LORA, 2 * D_RWKV), F32)
    w_lora = w_lora.at[:D_LORA, :D_RWKV].set(w_lora_up[0]).at[D_LORA:, D_RWKV:].set(a_lora_up[0])
    vecs = jnp.concatenate([
        w0[0][None], a0[0][None], k_k[0][None], k_a[0][None], r_k[0].reshape(1, D_RWKV),
        lnx_g[0][None], lnx_b[0][None], conv_w[0],
        jnp.zeros((16 - 10, D_RWKV), F32)], axis=0)
    lane_head = jnp.arange(LANES) // HEAD
    seg_ones = (lane_head[:, None] == lane_head[None, :]).astype(BF16)
    t_idx = jnp.arange(CHUNK)
    tri = (t_idx[None, :] <= t_idx[:, None]).astype(BF16)
    tri2 = jnp.concatenate([tri, tri], axis=1)

    mix = _mixer(proj_meta, lora_meta, proj, lora, mu_shift, vecs, w_lora.astype(BF16), seg_ones, tri2)

    out = _outproj(mix.reshape(bsz * seq, D_MODEL), x2d, w_out[0].astype(BF16), norm_f_g[None, :])
    return out.reshape(bsz, seq, D_MODEL)
```

```python
import math

import jax
import jax.numpy as jnp
from jax import lax
from jax.experimental import pallas as pl
from jax.experimental.pallas import tpu as pltpu

F32 = jnp.float32
BF16 = jnp.bfloat16

D_MODEL = 2048
N_META = 16
D_RWKV = 1024
HEAD = 64
N_HEADS = D_RWKV // HEAD
D_LORA = 64
D_CONV = 1024
RW_COLS = 4 * D_RWKV + 2 * D_LORA
IN_COLS = RW_COLS + 4 * D_CONV
LORA_COLS = 2 * D_LORA
MAIN_COLS = IN_COLS - LORA_COLS
RMS_EPS = 1e-6
LNX_EPS = 64e-5

CHUNK = 64
LANES = 128
SUBLANES = 8
N_PAIRS = D_RWKV // LANES
FRONT_PAD = CHUNK - N_META

VMEM_LIMIT = 56 * 1024 * 1024
PROJ_VMEM_LIMIT = 60 * 1024 * 1024

INPROJ_TM = 1024
INPROJ_TN = 2048
META_TN = 1664
OUTPROJ_TM = 1024


def _rmsnorm_bf16(x, g):
    ms = jnp.mean(x * x, axis=-1, keepdims=True)
    return (x * lax.rsqrt(ms + RMS_EPS) * g).astype(BF16)


def _inproj_body(x_ref, g_ref, w_ref, wl_ref, o_ref, ol_ref, hn_ref):
    @pl.when(pl.program_id(2) == 0)
    def _():
        hn_ref[...] = _rmsnorm_bf16(x_ref[0], g_ref[...])
        ol_ref[0] = jnp.dot(hn_ref[...], wl_ref[...], preferred_element_type=F32)

    o_ref[0] = jnp.dot(hn_ref[...], w_ref[...], preferred_element_type=F32)


def _main_col_start(j):
    skip = jnp.where(j * INPROJ_TN >= 4 * D_RWKV, LORA_COLS // LANES, 0)
    return (j * (INPROJ_TN // LANES) + skip) * LANES


def _inproj(x, g, w_bf16):
    bsz, seq, _ = x.shape
    tm, tn = INPROJ_TM, INPROJ_TN
    assert seq % tm == 0 and MAIN_COLS % tn == 0 and (4 * D_RWKV) % tn == 0
    return pl.pallas_call(
        _inproj_body,
        grid=(bsz, seq // tm, MAIN_COLS // tn),
        in_specs=[
            pl.BlockSpec((1, tm, D_MODEL), lambda b, i, j: (b, i, 0)),
            pl.BlockSpec((1, D_MODEL), lambda b, i, j: (0, 0)),
            pl.BlockSpec((pl.Element(D_MODEL), pl.Element(tn)), lambda b, i, j: (0, _main_col_start(j))),
            pl.BlockSpec((D_MODEL, LORA_COLS), lambda b, i, j: (0, 4 * D_RWKV // LORA_COLS)),
        ],
        out_specs=[pl.BlockSpec((1, tm, tn), lambda b, i, j: (b, i, j)),
                   pl.BlockSpec((1, tm, LORA_COLS), lambda b, i, j: (b, i, 0))],
        out_shape=[jax.ShapeDtypeStruct((bsz, seq, MAIN_COLS), F32),
                   jax.ShapeDtypeStruct((bsz, seq, LORA_COLS), F32)],
        scratch_shapes=[pltpu.VMEM((tm, D_MODEL), BF16)],
        compiler_params=pltpu.CompilerParams(
            dimension_semantics=("arbitrary", "arbitrary", "arbitrary"),
            vmem_limit_bytes=PROJ_VMEM_LIMIT),
        name="inproj",
    )(x, g, w_bf16, w_bf16)


def _inproj_meta_body(x_ref, g_ref, w_ref, o_ref, wb_ref):
    w = w_ref[...].astype(BF16)
    wb_ref[...] = w
    o_ref[0] = jnp.dot(_rmsnorm_bf16(x_ref[...], g_ref[...]), w, preferred_element_type=F32)


def _inproj_meta(front, g, w_f32):
    tn = META_TN
    assert IN_COLS % tn == 0
    return pl.pallas_call(
        _inproj_meta_body,
        grid=(IN_COLS // tn,),
        in_specs=[
            pl.BlockSpec((CHUNK, D_MODEL), lambda j: (0, 0)),
            pl.BlockSpec((1, D_MODEL), lambda j: (0, 0)),
            pl.BlockSpec((D_MODEL, tn), lambda j: (0, j)),
        ],
        out_specs=[pl.BlockSpec((1, CHUNK, tn), lambda j: (0, 0, j)),
                   pl.BlockSpec((D_MODEL, tn), lambda j: (0, j))],
        out_shape=[jax.ShapeDtypeStruct((1, CHUNK, IN_COLS), F32),
                   jax.ShapeDtypeStruct((D_MODEL, IN_COLS), BF16)],
        compiler_params=pltpu.CompilerParams(
            dimension_semantics=("arbitrary",),
            vmem_limit_bytes=VMEM_LIMIT),
        name="inproj_meta",
    )(front, g, w_f32)


class _Split:
    def __init__(self, x):
        self.x = x
        self._hi = None
        self._lo = None

    @property
    def hi(self):
        if self._hi is None:
            self._hi = self.x.astype(BF16)
        return self._hi

    @property
    def lo(self):
        if self._lo is None:
            self._lo = (self.x - self.hi.astype(F32)).astype(BF16)
        return self._lo


class _BlockDiag:
    def __init__(self, y):
        self._y = y
        self._hi = None

    @property
    def hi(self):
        if self._hi is None:
            yp = self._y.hi
            lane = lax.broadcasted_iota(jnp.int32, yp.shape, 1)
            first = lane < HEAD
            zero = jnp.zeros_like(yp)
            self._hi = jnp.concatenate([jnp.where(first, yp, zero), jnp.where(first, zero, yp)], axis=0)
        return self._hi


def _mm(terms):
    la = jnp.concatenate([a.hi for a, _ in terms], axis=1)
    lb = jnp.concatenate([b.hi for _, b in terms], axis=0)
    return jnp.dot(la, lb, preferred_element_type=F32)


def _shift_rows(x, carry, n):
    row = lax.broadcasted_iota(jnp.int32, x.shape, 0)
    out = pltpu.roll(x, n, axis=0)
    for i in range(n):
        src = SUBLANES - n + i
        out = jnp.where(row == i, carry[src:src + 1, :], out)
    return out


def _sigmoid(x):
    return 0.5 + 0.5 * jnp.tanh(0.5 * x)


def _silu(x):
    h = 0.5 * x
    return h + h * jnp.tanh(h)


def _interleave(order, stages):
    for name in order:
        next(stages[name], None)
    for gen in stages.values():
        for _ in gen:
            pass


_HALF_ORDER = ("inverse", "inverse", "advance", "inverse", "advance", "inverse", "advance", "finish", "inverse",
               "advance", "finish", "inverse", "prep", "finish", "inverse", "prep", "inverse", "inverse", "prep",
               "inverse", "prep", "inverse")


def _mixer_body(pm_ref, lm_ref, px_ref, lx_ref, mu_ref, vec_ref, wl_ref, ones_ref, tri_ref, o_ref,
                carry_rw, carry_u, state, s_at, s_rt, s_kh, s_bh, s_v, s_pe, s_y,
                r_x, r_tak, r_ark, r_arb, e_bonus, e_gate, e_yb):
    def vec(i):
        return vec_ref[i:i + 1, :]

    w0, a0, k_k, k_a, r_k, lnx_g, lnx_b, cw0, cw1, cw2 = (vec(i) for i in range(10))
    ps = range(N_PAIRS)

    def seg_sums(xs):
        tiles = [x[:, p * LANES:(p + 1) * LANES] for x in xs for p in ps]
        t = jnp.concatenate(tiles, axis=0).astype(BF16)
        s = jnp.dot(t, ones_ref[...], preferred_element_type=F32)
        out = []
        for i in range(len(xs)):
            rows = [s[(i * N_PAIRS + p) * CHUNK:(i * N_PAIRS + p + 1) * CHUNK] for p in ps]
            out.append(jnp.concatenate(rows, axis=1))
        return out

    def prep(p_ref, l_ref, row0, sel):
        p_rw = jnp.concatenate([p_ref[0, row0:row0 + CHUNK, :4 * D_RWKV], l_ref[0, row0:row0 + CHUNK, :]],
                               axis=1)
        prev = _shift_rows(p_rw, carry_rw[...], 1)
        carry_rw[...] = p_rw[CHUNK - SUBLANES:, :]
        pm = p_rw + (prev - p_rw) * mu_ref[...]
        r = pm[:, 0 * D_RWKV:1 * D_RWKV]
        k = pm[:, 1 * D_RWKV:2 * D_RWKV]
        v = pm[:, 2 * D_RWKV:3 * D_RWKV]
        g_r = pm[:, 3 * D_RWKV:4 * D_RWKV]
        lora_in = pm[:, 4 * D_RWKV:]
        lane = lax.broadcasted_iota(jnp.int32, lora_in.shape, 1)
        lora_in = jnp.where(lane < D_LORA, jnp.tanh(lora_in), lora_in).astype(BF16)
        yield
        lora = jnp.dot(lora_in, wl_ref[...], preferred_element_type=F32)
        logw = (-math.exp(-0.5)) * _sigmoid(w0 + lora[:, :D_RWKV])
        a = _sigmoid(a0 + lora[:, D_RWKV:])
        yield
        logw = _Split(logw)
        cum = jnp.dot(tri_ref[...], jnp.concatenate([logw.hi, logw.lo], axis=0),
                      preferred_element_type=F32)
        logw = logw.x
        kk = k * k_k
        k = k * (1.0 + (a - 1.0) * k_a)
        yield
        kk_sq, rk_sum = seg_sums([kk * kk, r * k * r_k])
        kk = kk * lax.rsqrt(jnp.maximum(kk_sq, 1e-24))
        beta = kk * a
        e_in = jnp.exp(cum)
        e_out = jnp.exp(-cum)
        e_ex = jnp.exp(cum - logw)
        p_end = e_in[CHUNK - 1:CHUNK, :]
        a_t = kk * e_ex
        r_t = r * e_in
        k_h = k * e_out
        b_h = beta * e_out
        for dst, val in ((s_at, a_t), (s_rt, r_t), (s_kh, k_h), (s_bh, b_h), (s_v, v)):
            val = val.astype(dst.dtype)
            for p in ps:
                dst[sel, p] = val[:, p * LANES:(p + 1) * LANES]
        for p in ps:
            s_pe[sel, p] = jnp.broadcast_to(p_end[:, p * LANES:(p + 1) * LANES], (SUBLANES, LANES))
        e_bonus[sel] = rk_sum * v
        e_gate[sel] = _silu(g_r)

        p_cv = p_ref[0, row0:row0 + CHUNK, 4 * D_RWKV:]
        b_g = p_cv[:, 0 * D_CONV:1 * D_CONV]
        c_g = p_cv[:, 1 * D_CONV:2 * D_CONV]
        h_c = p_cv[:, 2 * D_CONV:3 * D_CONV]
        g_c = p_cv[:, 3 * D_CONV:4 * D_CONV]
        u_c = c_g * h_c
        carry = carry_u[...]
        conv = cw0 * _shift_rows(u_c, carry, 2) + cw1 * _shift_rows(u_c, carry, 1) + cw2 * u_c
        carry_u[...] = u_c[CHUNK - SUBLANES:, :]
        e_yb[sel] = (b_g * conv * _silu(g_c)).astype(e_yb.dtype)

    t_i = lax.broadcasted_iota(jnp.int32, (CHUNK, LANES), 0)
    j_i = lax.broadcasted_iota(jnp.int32, (CHUNK, LANES), 1) & (HEAD - 1)

    def splits(xs):
        return [_Split(x) for x in xs]

    def pmm(xs, ys):
        return [_mm([(x, _BlockDiag(y))]) for x, y in zip(xs, ys)]

    def inverse(sel):
        strict = j_i < t_i
        incl = j_i <= t_i
        eye = (j_i == t_i).astype(F32)

        def same_block(log2_size):
            return (j_i >> log2_size) == (t_i >> log2_size)

        g = []
        for p in ps:
            lhs = jnp.concatenate([s_at[sel, p], s_rt[sel, p]], axis=0)
            rhs_t = jnp.concatenate([_BlockDiag(_Split(s_bh[sel, p])).hi,
                                     _BlockDiag(_Split(s_kh[sel, p])).hi], axis=0)
            g.append(lax.dot_general(lhs, rhs_t, (((1,), (1,)), ((), ())), preferred_element_type=F32))
        t_ab = [jnp.where(strict, g_[:CHUNK, :LANES], 0.0) for g_ in g]
        for p in ps:
            r_arb[sel, p] = jnp.where(incl, g[p][CHUNK:, :LANES], 0.0).astype(r_arb.dtype)
            r_tak[sel, p] = jnp.where(strict, g[p][:CHUNK, LANES:], 0.0).astype(r_tak.dtype)
            r_ark[sel, p] = jnp.where(incl, g[p][CHUNK:, LANES:], 0.0).astype(r_ark.dtype)
        yield

        x = splits([eye - jnp.where(same_block(1), t, 0.0) for t in t_ab])
        for log2_size in range(2, 7):
            level = same_block(log2_size) & jnp.logical_not(same_block(log2_size - 1))
            c = splits([jnp.where(level, t, 0.0) for t in t_ab])
            xc = splits(pmm(x, c))
            yield
            x = splits([x_.x - m for x_, m in zip(x, pmm(xc, x))])
            if log2_size < 6:
                yield
        for p in ps:
            r_x[sel, p] = x[p].hi

    def advance(sel):
        row = lax.broadcasted_iota(jnp.int32, (LANES, LANES), 0)
        col = lax.broadcasted_iota(jnp.int32, (LANES, LANES), 1)
        same_head = (row < HEAD) == (col < HEAD)
        vv = [s_v[sel, p] for p in ps]
        vbd = [_BlockDiag(_Split(v_)) for v_ in vv]
        s0 = [state[p] for p in ps]
        h0 = splits([s.T for s in s0])
        rhs = splits([_mm([(_Split(s_at[sel, p]), h0[p]), (_Split(r_tak[sel, p]), vbd[p])]) for p in ps])
        yield
        u = splits(pmm(splits([r_x[sel, p] for p in ps]), rhs))
        yield
        for p in ps:
            s_y[sel, p] = _mm([(_Split(s_rt[sel, p]), h0[p]), (_Split(r_ark[sel, p]), vbd[p]),
                               (_Split(-r_arb[sel, p]), _BlockDiag(u[p]))])
        yield
        for p in ps:
            wv_t = _Split(jnp.concatenate([vv[p], u[p].x], axis=0).T)
            upd = _mm([(wv_t, _Split(jnp.concatenate([s_kh[sel, p], -s_bh[sel, p]], axis=0)))])
            state[p] = jnp.where(same_head, (s0[p] + upd) * s_pe[sel, p][0:1, :], 0.0)

    def finish(sel, row0):
        y = jnp.concatenate([s_y[sel, p] for p in ps], axis=1)
        inv_n = 1.0 / HEAD
        yield
        mean = seg_sums([y])[0] * inv_n
        yc = y - mean
        yield
        var = seg_sums([yc * yc])[0] * inv_n
        yn = yc * lax.rsqrt(var + LNX_EPS) * lnx_g + lnx_b
        y_a = (yn + e_bonus[sel]) * e_gate[sel]
        o_ref[0, row0:row0 + CHUNK, :D_RWKV] = y_a.astype(o_ref.dtype)
        o_ref[0, row0:row0 + CHUNK, D_RWKV:] = e_yb[sel]

    @pl.when(pl.program_id(1) == 0)
    def _():
        for ref in (carry_rw, carry_u, state, s_at, s_rt, s_kh, s_bh, s_v, s_pe,
                    r_x, r_tak, r_ark, r_arb, e_bonus, e_gate, e_yb):
            ref[...] = jnp.zeros_like(ref)
        for _ in prep(pm_ref, lm_ref, 0, 0):
            pass

    _interleave(_HALF_ORDER, {"advance": advance(1), "finish": finish(1, 0), "inverse": inverse(0),
                              "prep": prep(px_ref, lx_ref, 0, 1)})
    _interleave(_HALF_ORDER, {"advance": advance(0), "finish": finish(0, CHUNK), "inverse": inverse(1),
                              "prep": prep(px_ref, lx_ref, CHUNK, 0)})


def _mixer(proj_meta, lora_meta, proj, lora, mu, vecs, w_lora, seg_ones, tri2):
    bsz, seq, _ = proj.shape
    assert seq % (2 * CHUNK) == 0
    n_steps = seq // (2 * CHUNK) + 1
    pair_buf = pltpu.VMEM((2, N_PAIRS, CHUNK, LANES), F32)
    pair_bf16 = pltpu.VMEM((2, N_PAIRS, CHUNK, LANES), BF16)
    row_buf = pltpu.VMEM((2, CHUNK, D_RWKV), F32)
    const = lambda shape: pl.BlockSpec(shape, lambda b, j: (0,) * len(shape))
    return pl.pallas_call(
        _mixer_body,
        grid=(bsz, n_steps),
        in_specs=[
            const(proj_meta.shape), const(lora_meta.shape),
            pl.BlockSpec((1, 2 * CHUNK, MAIN_COLS), lambda b, j: (b, jnp.minimum(j, n_steps - 2), 0)),
            pl.BlockSpec((1, 2 * CHUNK, LORA_COLS), lambda b, j: (b, jnp.minimum(j, n_steps - 2), 0)),
            const(mu.shape), const(vecs.shape), const(w_lora.shape),
            const(seg_ones.shape), const(tri2.shape),
        ],
        out_specs=pl.BlockSpec((1, 2 * CHUNK, D_MODEL), lambda b, j: (b, jnp.maximum(j - 1, 0), 0)),
        out_shape=jax.ShapeDtypeStruct((bsz, seq, D_MODEL), BF16),
        scratch_shapes=[
            pltpu.VMEM((SUBLANES, RW_COLS), F32),
            pltpu.VMEM((SUBLANES, D_CONV), F32),
            pltpu.VMEM((N_PAIRS, LANES, LANES), F32),
            pair_bf16, pair_bf16, pair_bf16, pair_bf16, pair_buf,
            pltpu.VMEM((2, N_PAIRS, SUBLANES, LANES), F32),
            pair_buf,
            pair_bf16, pair_bf16, pair_bf16, pair_bf16,
            row_buf, row_buf,
            pltpu.VMEM((2, CHUNK, D_CONV), BF16),
        ],
        compiler_params=pltpu.CompilerParams(
            dimension_semantics=("arbitrary", "arbitrary"),
            vmem_limit_bytes=VMEM_LIMIT),
        name="mixer",
    )(proj_meta, lora_meta, proj, lora, mu, vecs, w_lora, seg_ones, tri2)


def _outproj_body(mix_ref, x_ref, w_ref, g_ref, o_ref):
    h = x_ref[...] + jnp.dot(mix_ref[...], w_ref[...], preferred_element_type=F32)
    ms = jnp.mean(h * h, axis=-1, keepdims=True)
    o_ref[...] = h * lax.rsqrt(ms + RMS_EPS) * g_ref[...]


def _outproj(mix2d, x2d, w_bf16, g):
    rows = x2d.shape[0]
    tm = OUTPROJ_TM
    assert rows % tm == 0
    return pl.pallas_call(
        _outproj_body,
        grid=(rows // tm,),
        in_specs=[
            pl.BlockSpec((tm, D_MODEL), lambda i: (i, 0)),
            pl.BlockSpec((tm, D_MODEL), lambda i: (i, 0)),
            pl.BlockSpec((D_MODEL, D_MODEL), lambda i: (0, 0), pipeline_mode=pl.Buffered(1)),
            pl.BlockSpec((1, D_MODEL), lambda i: (0, 0)),
        ],
        out_specs=pl.BlockSpec((tm, D_MODEL), lambda i: (i, 0)),
        out_shape=jax.ShapeDtypeStruct((rows, D_MODEL), F32),
        compiler_params=pltpu.CompilerParams(
            dimension_semantics=("arbitrary",),
            vmem_limit_bytes=PROJ_VMEM_LIMIT),
        name="outproj",
    )(mix2d, x2d, w_bf16, g)


def kernel(x, meta_tokens, norm_in_g, w_in, mu_shift, w0, w_lora_up, a0, a_lora_up, k_k, k_a, r_k, lnx_g, lnx_b, conv_w, w_out, norm_f_g):
    bsz, seq, _ = x.shape
    assert norm_in_g.shape[0] == 1 and seq % CHUNK == 0
    x2d = x.reshape(bsz * seq, D_MODEL)

    front = jnp.concatenate([jnp.zeros((FRONT_PAD, D_MODEL), x.dtype), meta_tokens.astype(x.dtype)], axis=0)
    g_in = norm_in_g[0][None, :]
    meta_all, w_in_bf16 = _inproj_meta(front, g_in, w_in[0])
    proj_meta = jnp.concatenate([meta_all[..., :4 * D_RWKV], meta_all[..., RW_COLS:]], axis=-1)
    lora_meta = meta_all[..., 4 * D_RWKV:RW_COLS]
    proj, lora = _inproj(x, g_in, w_in_bf16)

    w_lora = jnp.zeros((2 * D_LORA, 2 * D_RWKV), F32)
    w_lora = w_lora.at[:D_LORA, :D_RWKV].set(w_lora_up[0]).at[D_LORA:, D_RWKV:].set(a_lora_up[0])
    vecs = jnp.concatenate([
        w0[0][None], a0[0][None], k_k[0][None], k_a[0][None], r_k[0].reshape(1, D_RWKV),
        lnx_g[0][None], lnx_b[0][None], conv_w[0],
        jnp.zeros((16 - 10, D_RWKV), F32)], axis=0)
    lane_head = jnp.arange(LANES) // HEAD
    seg_ones = (lane_head[:, None] == lane_head[None, :]).astype(BF16)
    t_idx = jnp.arange(CHUNK)
    tri = (t_idx[None, :] <= t_idx[:, None]).astype(BF16)
    tri2 = jnp.concatenate([tri, tri], axis=1)

    mix = _mixer(proj_meta, lora_meta, proj, lora, mu_shift, vecs, w_lora.astype(BF16), seg_ones, tri2)

    out = _outproj(mix.reshape(bsz * seq, D_MODEL), x2d, w_out[0].astype(BF16), norm_f_g[None, :])
    return out.reshape(bsz, seq, D_MODEL)
```

```python
import math

import jax
import jax.numpy as jnp
from jax import lax
from jax.experimental import pallas as pl
from jax.experimental.pallas import tpu as pltpu

F32 = jnp.float32
BF16 = jnp.bfloat16

D_MODEL = 2048
N_META = 16
D_RWKV = 1024
HEAD = 64
N_HEADS = D_RWKV // HEAD
D_LORA = 64
D_CONV = 1024
RW_COLS = 4 * D_RWKV + 2 * D_LORA
IN_COLS = RW_COLS + 4 * D_CONV
LORA_COLS = 2 * D_LORA
MAIN_COLS = IN_COLS - LORA_COLS
RMS_EPS = 1e-6
LNX_EPS = 64e-5

CHUNK = 64
LANES = 128
SUBLANES = 8
N_PAIRS = D_RWKV // LANES
FRONT_PAD = CHUNK - N_META
ENTRIES_PER_STEP = 2

VMEM_LIMIT = 56 * 1024 * 1024
PROJ_VMEM_LIMIT = 60 * 1024 * 1024

INPROJ_TM = 1024
INPROJ_TN = 2048
META_TN = 1664
OUTPROJ_TM = 1024


def _rmsnorm_bf16(x, g):
    ms = jnp.mean(x * x, axis=-1, keepdims=True)
    return (x * lax.rsqrt(ms + RMS_EPS) * g).astype(BF16)


def _inproj_body(x_ref, g_ref, w_ref, wl_ref, o_ref, ol_ref, hn_ref):
    @pl.when(pl.program_id(2) == 0)
    def _():
        hn_ref[...] = _rmsnorm_bf16(x_ref[0], g_ref[...])
        ol_ref[0] = jnp.dot(hn_ref[...], wl_ref[...], preferred_element_type=F32)

    o_ref[0] = jnp.dot(hn_ref[...], w_ref[...], preferred_element_type=F32)


def _main_col_start(j):
    skip = jnp.where(j * INPROJ_TN >= 4 * D_RWKV, LORA_COLS // LANES, 0)
    return (j * (INPROJ_TN // LANES) + skip) * LANES


def _inproj(x, g, w_bf16):
    bsz, seq, _ = x.shape
    tm, tn = INPROJ_TM, INPROJ_TN
    assert seq % tm == 0 and MAIN_COLS % tn == 0 and (4 * D_RWKV) % tn == 0
    return pl.pallas_call(
        _inproj_body,
        grid=(bsz, seq // tm, MAIN_COLS // tn),
        in_specs=[
            pl.BlockSpec((1, tm, D_MODEL), lambda b, i, j: (b, i, 0)),
            pl.BlockSpec((1, D_MODEL), lambda b, i, j: (0, 0)),
            pl.BlockSpec((pl.Element(D_MODEL), pl.Element(tn)), lambda b, i, j: (0, _main_col_start(j))),
            pl.BlockSpec((D_MODEL, LORA_COLS), lambda b, i, j: (0, 4 * D_RWKV // LORA_COLS)),
        ],
        out_specs=[pl.BlockSpec((1, tm, tn), lambda b, i, j: (b, i, j)),
                   pl.BlockSpec((1, tm, LORA_COLS), lambda b, i, j: (b, i, 0))],
        out_shape=[jax.ShapeDtypeStruct((bsz, seq, MAIN_COLS), F32),
                   jax.ShapeDtypeStruct((bsz, seq, LORA_COLS), F32)],
        scratch_shapes=[pltpu.VMEM((tm, D_MODEL), BF16)],
        compiler_params=pltpu.CompilerParams(
            dimension_semantics=("arbitrary", "arbitrary", "arbitrary"),
            vmem_limit_bytes=PROJ_VMEM_LIMIT),
        name="inproj",
    )(x, g, w_bf16, w_bf16)


def _inproj_meta_body(x_ref, g_ref, w_ref, o_ref, wb_ref):
    w = w_ref[...].astype(BF16)
    wb_ref[...] = w
    o_ref[0] = jnp.dot(_rmsnorm_bf16(x_ref[...], g_ref[...]), w, preferred_element_type=F32)


def _inproj_meta(front, g, w_f32):
    tn = META_TN
    assert IN_COLS % tn == 0
    return pl.pallas_call(
        _inproj_meta_body,
        grid=(IN_COLS // tn,),
        in_specs=[
            pl.BlockSpec((CHUNK, D_MODEL), lambda j: (0, 0)),
            pl.BlockSpec((1, D_MODEL), lambda j: (0, 0)),
            pl.BlockSpec((D_MODEL, tn), lambda j: (0, j)),
        ],
        out_specs=[pl.BlockSpec((1, CHUNK, tn), lambda j: (0, 0, j)),
                   pl.BlockSpec((D_MODEL, tn), lambda j: (0, j))],
        out_shape=[jax.ShapeDtypeStruct((1, CHUNK, IN_COLS), F32),
                   jax.ShapeDtypeStruct((D_MODEL, IN_COLS), BF16)],
        compiler_params=pltpu.CompilerParams(
            dimension_semantics=("arbitrary",),
            vmem_limit_bytes=VMEM_LIMIT),
        name="inproj_meta",
    )(front, g, w_f32)


class _Split:
    def __init__(self, x):
        self.x = x
        self._hi = None
        self._lo = None

    @property
    def hi(self):
        if self._hi is None:
            self._hi = self.x.astype(BF16)
        return self._hi

    @property
    def lo(self):
        if self._lo is None:
            self._lo = (self.x - self.hi.astype(F32)).astype(BF16)
        return self._lo


class _BlockDiag:
    def __init__(self, y):
        self._y = y
        self._hi = None

    @property
    def hi(self):
        if self._hi is None:
            yp = self._y.hi
            lane = lax.broadcasted_iota(jnp.int32, yp.shape, 1)
            first = lane < HEAD
            zero = jnp.zeros_like(yp)
            self._hi = jnp.concatenate([jnp.where(first, yp, zero), jnp.where(first, zero, yp)], axis=0)
        return self._hi


def _mm(terms):
    la = jnp.concatenate([a.hi for a, _ in terms], axis=1)
    lb = jnp.concatenate([b.hi for _, b in terms], axis=0)
    return jnp.dot(la, lb, preferred_element_type=F32)


def _shift_rows(x, carry, n):
    row = lax.broadcasted_iota(jnp.int32, x.shape, 0)
    out = pltpu.roll(x, n, axis=0)
    for i in range(n):
        src = SUBLANES - n + i
        out = jnp.where(row == i, carry[src:src + 1, :], out)
    return out


def _sigmoid(x):
    return 0.5 + 0.5 * jnp.tanh(0.5 * x)


def _silu(x):
    h = 0.5 * x
    return h + h * jnp.tanh(h)


def _interleave(order, stages):
    for name in order:
        next(stages[name], None)
    for gen in stages.values():
        for _ in gen:
            pass


_HALF_ORDER = ("inverse", "inverse", "advance", "inverse", "advance", "inverse", "advance", "finish", "inverse",
               "advance", "finish", "inverse", "prep", "finish", "inverse", "prep", "inverse", "inverse", "prep",
               "inverse", "prep", "inverse")


def _mixer_body(pm_ref, lm_ref, px_ref, lx_ref, mu_ref, vec_ref, wl_ref, ones_ref, tri_ref, o_ref, *scratch):
    for e in range(ENTRIES_PER_STEP):
        one = pl.ds(e, 1)
        _mixer_entry(pm_ref, lm_ref, px_ref.at[one], lx_ref.at[one], mu_ref, vec_ref, wl_ref, ones_ref, tri_ref,
                     o_ref.at[one], *[s.at[e] for s in scratch])


def _mixer_entry(pm_ref, lm_ref, px_ref, lx_ref, mu_ref, vec_ref, wl_ref, ones_ref, tri_ref, o_ref,
                carry_rw, carry_u, state, s_at, s_rt, s_kh, s_bh, s_v, s_pe, s_y,
                r_x, r_tak, r_ark, r_arb, e_bonus, e_gate, e_yb):
    def vec(i):
        return vec_ref[i:i + 1, :]

    w0, a0, k_k, k_a, r_k, lnx_g, lnx_b, cw0, cw1, cw2 = (vec(i) for i in range(10))
    ps = range(N_PAIRS)

    def seg_sums(xs):
        tiles = [x[:, p * LANES:(p + 1) * LANES] for x in xs for p in ps]
        t = jnp.concatenate(tiles, axis=0).astype(BF16)
        s = jnp.dot(t, ones_ref[...], preferred_element_type=F32)
        out = []
        for i in range(len(xs)):
            rows = [s[(i * N_PAIRS + p) * CHUNK:(i * N_PAIRS + p + 1) * CHUNK] for p in ps]
            out.append(jnp.concatenate(rows, axis=1))
        return out

    def prep(p_ref, l_ref, row0, sel):
        p_rw = jnp.concatenate([p_ref[0, row0:row0 + CHUNK, :4 * D_RWKV], l_ref[0, row0:row0 + CHUNK, :]],
                               axis=1)
        prev = _shift_rows(p_rw, carry_rw[...], 1)
        carry_rw[...] = p_rw[CHUNK - SUBLANES:, :]
        pm = p_rw + (prev - p_rw) * mu_ref[...]
        r = pm[:, 0 * D_RWKV:1 * D_RWKV]
        k = pm[:, 1 * D_RWKV:2 * D_RWKV]
        v = pm[:, 2 * D_RWKV:3 * D_RWKV]
        g_r = pm[:, 3 * D_RWKV:4 * D_RWKV]
        lora_in = pm[:, 4 * D_RWKV:]
        lane = lax.broadcasted_iota(jnp.int32, lora_in.shape, 1)
        lora_in = jnp.where(lane < D_LORA, jnp.tanh(lora_in), lora_in).astype(BF16)
        yield
        lora = jnp.dot(lora_in, wl_ref[...], preferred_element_type=F32)
        logw = (-math.exp(-0.5)) * _sigmoid(w0 + lora[:, :D_RWKV])
        a = _sigmoid(a0 + lora[:, D_RWKV:])
        yield
        logw = _Split(logw)
        cum = jnp.dot(tri_ref[...], jnp.concatenate([logw.hi, logw.lo], axis=0),
                      preferred_element_type=F32)
        logw = logw.x
        kk = k * k_k
        k = k * (1.0 + (a - 1.0) * k_a)
        yield
        kk_sq, rk_sum = seg_sums([kk * kk, r * k * r_k])
        kk = kk * lax.rsqrt(jnp.maximum(kk_sq, 1e-24))
        beta = kk * a
        e_in = jnp.exp(cum)
        e_out = jnp.exp(-cum)
        e_ex = jnp.exp(cum - logw)
        p_end = e_in[CHUNK - 1:CHUNK, :]
        a_t = kk * e_ex
        r_t = r * e_in
        k_h = k * e_out
        b_h = beta * e_out
        for dst, val in ((s_at, a_t), (s_rt, r_t), (s_kh, k_h), (s_bh, b_h), (s_v, v)):
            val = val.astype(dst.dtype)
            for p in ps:
                dst[sel, p] = val[:, p * LANES:(p + 1) * LANES]
        for p in ps:
            s_pe[sel, p] = jnp.broadcast_to(p_end[:, p * LANES:(p + 1) * LANES], (SUBLANES, LANES))
        e_bonus[sel] = rk_sum * v
        e_gate[sel] = _silu(g_r)

        p_cv = p_ref[0, row0:row0 + CHUNK, 4 * D_RWKV:]
        b_g = p_cv[:, 0 * D_CONV:1 * D_CONV]
        c_g = p_cv[:, 1 * D_CONV:2 * D_CONV]
        h_c = p_cv[:, 2 * D_CONV:3 * D_CONV]
        g_c = p_cv[:, 3 * D_CONV:4 * D_CONV]
        u_c = c_g * h_c
        carry = carry_u[...]
        conv = cw0 * _shift_rows(u_c, carry, 2) + cw1 * _shift_rows(u_c, carry, 1) + cw2 * u_c
        carry_u[...] = u_c[CHUNK - SUBLANES:, :]
        e_yb[sel] = (b_g * conv * _silu(g_c)).astype(e_yb.dtype)

    t_i = lax.broadcasted_iota(jnp.int32, (CHUNK, LANES), 0)
    j_i = lax.broadcasted_iota(jnp.int32, (CHUNK, LANES), 1) & (HEAD - 1)

    def splits(xs):
        return [_Split(x) for x in xs]

    def pmm(xs, ys):
        return [_mm([(x, _BlockDiag(y))]) for x, y in zip(xs, ys)]

    def inverse(sel):
        strict = j_i < t_i
        incl = j_i <= t_i
        eye = (j_i == t_i).astype(F32)

        def same_block(log2_size):
            return (j_i >> log2_size) == (t_i >> log2_size)

        g = []
        for p in ps:
            lhs = jnp.concatenate([s_at[sel, p], s_rt[sel, p]], axis=0)
            rhs_t = jnp.concatenate([_BlockDiag(_Split(s_bh[sel, p])).hi,
                                     _BlockDiag(_Split(s_kh[sel, p])).hi], axis=0)
            g.append(lax.dot_general(lhs, rhs_t, (((1,), (1,)), ((), ())), preferred_element_type=F32))
        t_ab = [jnp.where(strict, g_[:CHUNK, :LANES], 0.0) for g_ in g]
        for p in ps:
            r_arb[sel, p] = jnp.where(incl, g[p][CHUNK:, :LANES], 0.0).astype(r_arb.dtype)
            r_tak[sel, p] = jnp.where(strict, g[p][:CHUNK, LANES:], 0.0).astype(r_tak.dtype)
            r_ark[sel, p] = jnp.where(incl, g[p][CHUNK:, LANES:], 0.0).astype(r_ark.dtype)
        yield

        x = splits([eye - jnp.where(same_block(1), t, 0.0) for t in t_ab])
        for log2_size in range(2, 7):
            level = same_block(log2_size) & jnp.logical_not(same_block(log2_size - 1))
            c = splits([jnp.where(level, t, 0.0) for t in t_ab])
            xc = splits(pmm(x, c))
            yield
            x = splits([x_.x - m for x_, m in zip(x, pmm(xc, x))])
            if log2_size < 6:
                yield
        for p in ps:
            r_x[sel, p] = x[p].hi

    def advance(sel):
        row = lax.broadcasted_iota(jnp.int32, (LANES, LANES), 0)
        col = lax.broadcasted_iota(jnp.int32, (LANES, LANES), 1)
        same_head = (row < HEAD) == (col < HEAD)
        vv = [s_v[sel, p] for p in ps]
        vbd = [_BlockDiag(_Split(v_)) for v_ in vv]
        s0 = [state[p] for p in ps]
        h0 = splits([s.T for s in s0])
        rhs = splits([_mm([(_Split(s_at[sel, p]), h0[p]), (_Split(r_tak[sel, p]), vbd[p])]) for p in ps])
        yield
        u = splits(pmm(splits([r_x[sel, p] for p in ps]), rhs))
        yield
        for p in ps:
            s_y[sel, p] = _mm([(_Split(s_rt[sel, p]), h0[p]), (_Split(r_ark[sel, p]), vbd[p]),
                               (_Split(-r_arb[sel, p]), _BlockDiag(u[p]))])
        yield
        for p in ps:
            wv_t = _Split(jnp.concatenate([vv[p], u[p].x], axis=0).T)
            upd = _mm([(wv_t, _Split(jnp.concatenate([s_kh[sel, p], -s_bh[sel, p]], axis=0)))])
            state[p] = jnp.where(same_head, (s0[p] + upd) * s_pe[sel, p][0:1, :], 0.0)

    def finish(sel, row0):
        y = jnp.concatenate([s_y[sel, p] for p in ps], axis=1)
        inv_n = 1.0 / HEAD
        yield
        mean = seg_sums([y])[0] * inv_n
        yc = y - mean
        yield
        var = seg_sums([yc * yc])[0] * inv_n
        yn = yc * lax.rsqrt(var + LNX_EPS) * lnx_g + lnx_b
        y_a = (yn + e_bonus[sel]) * e_gate[sel]
        o_ref[0, row0:row0 + CHUNK, :D_RWKV] = y_a.astype(o_ref.dtype)
        o_ref[0, row0:row0 + CHUNK, D_RWKV:] = e_yb[sel]

    @pl.when(pl.program_id(1) == 0)
    def _():
        for ref in (carry_rw, carry_u, state, s_at, s_rt, s_kh, s_bh, s_v, s_pe,
                    r_x, r_tak, r_ark, r_arb, e_bonus, e_gate, e_yb):
            ref[...] = jnp.zeros_like(ref)
        for _ in prep(pm_ref, lm_ref, 0, 0):
            pass

    _interleave(_HALF_ORDER, {"advance": advance(1), "finish": finish(1, 0), "inverse": inverse(0),
                              "prep": prep(px_ref, lx_ref, 0, 1)})
    _interleave(_HALF_ORDER, {"advance": advance(0), "finish": finish(0, CHUNK), "inverse": inverse(1),
                              "prep": prep(px_ref, lx_ref, CHUNK, 0)})


def _mixer(proj_meta, lora_meta, proj, lora, mu, vecs, w_lora, seg_ones, tri2):
    bsz, seq, _ = proj.shape
    ne = ENTRIES_PER_STEP
    assert seq % (2 * CHUNK) == 0 and bsz % ne == 0
    n_steps = seq // (2 * CHUNK) + 1
    pair_buf = ((2, N_PAIRS, CHUNK, LANES), F32)
    pair_bf16 = ((2, N_PAIRS, CHUNK, LANES), BF16)
    row_buf = ((2, CHUNK, D_RWKV), F32)
    entry_scratch = [
        ((SUBLANES, RW_COLS), F32),
        ((SUBLANES, D_CONV), F32),
        ((N_PAIRS, LANES, LANES), F32),
        pair_bf16, pair_bf16, pair_bf16, pair_bf16, pair_buf,
        ((2, N_PAIRS, SUBLANES, LANES), F32),
        pair_buf,
        pair_bf16, pair_bf16, pair_bf16, pair_bf16,
        row_buf, row_buf,
        ((2, CHUNK, D_CONV), BF16),
    ]
    const = lambda shape: pl.BlockSpec(shape, lambda b, j: (0,) * len(shape))
    return pl.pallas_call(
        _mixer_body,
        grid=(bsz // ne, n_steps),
        in_specs=[
            const(proj_meta.shape), const(lora_meta.shape),
            pl.BlockSpec((ne, 2 * CHUNK, MAIN_COLS), lambda b, j: (b, jnp.minimum(j, n_steps - 2), 0)),
            pl.BlockSpec((ne, 2 * CHUNK, LORA_COLS), lambda b, j: (b, jnp.minimum(j, n_steps - 2), 0)),
            const(mu.shape), const(vecs.shape), const(w_lora.shape),
            const(seg_ones.shape), const(tri2.shape),
        ],
        out_specs=pl.BlockSpec((ne, 2 * CHUNK, D_MODEL), lambda b, j: (b, jnp.maximum(j - 1, 0), 0)),
        out_shape=jax.ShapeDtypeStruct((bsz, seq, D_MODEL), BF16),
        scratch_shapes=[pltpu.VMEM((ne,) + shape, dtype) for shape, dtype in entry_scratch],
        compiler_params=pltpu.CompilerParams(
            dimension_semantics=("arbitrary", "arbitrary"),
            vmem_limit_bytes=VMEM_LIMIT),
        name="mixer",
    )(proj_meta, lora_meta, proj, lora, mu, vecs, w_lora, seg_ones, tri2)


def _outproj_body(mix_ref, x_ref, w_ref, g_ref, o_ref):
    h = x_ref[...] + jnp.dot(mix_ref[...], w_ref[...], preferred_element_type=F32)
    ms = jnp.mean(h * h, axis=-1, keepdims=True)
    o_ref[...] = h * lax.rsqrt(ms + RMS_EPS) * g_ref[...]


def _outproj(mix2d, x2d, w_bf16, g):
    rows = x2d.shape[0]
    tm = OUTPROJ_TM
    assert rows % tm == 0
    return pl.pallas_call(
        _outproj_body,
        grid=(rows // tm,),
        in_specs=[
            pl.BlockSpec((tm, D_MODEL), lambda i: (i, 0)),
            pl.BlockSpec((tm, D_MODEL), lambda i: (i, 0)),
            pl.BlockSpec((D_MODEL, D_MODEL), lambda i: (0, 0), pipeline_mode=pl.Buffered(1)),
            pl.BlockSpec((1, D_MODEL), lambda i: (0, 0)),
        ],
        out_specs=pl.BlockSpec((tm, D_MODEL), lambda i: (i, 0)),
        out_shape=jax.ShapeDtypeStruct((rows, D_MODEL), F32),
        compiler_params=pltpu.CompilerParams(
            dimension_semantics=("arbitrary",),
            vmem_limit_bytes=PROJ_VMEM_LIMIT),
        name="outproj",
    )(mix2d, x2d, w_bf16, g)


def kernel(x, meta_tokens, norm_in_g, w_in, mu_shift, w0, w_lora_up, a0, a_lora_up, k_k, k_a, r_k, lnx_g, lnx_b, conv_w, w_out, norm_f_g):
    bsz, seq, _ = x.shape
    assert norm_in_g.shape[0] == 1 and seq % CHUNK == 0
    x2d = x.reshape(bsz * seq, D_MODEL)

    front = jnp.concatenate([jnp.zeros((FRONT_PAD, D_MODEL), x.dtype), meta_tokens.astype(x.dtype)], axis=0)
    g_in = norm_in_g[0][None, :]
    meta_all, w_in_bf16 = _inproj_meta(front, g_in, w_in[0])
    proj_meta = jnp.concatenate([meta_all[..., :4 * D_RWKV], meta_all[..., RW_COLS:]], axis=-1)
    lora_meta = meta_all[..., 4 * D_RWKV:RW_COLS]
    proj, lora = _inproj(x, g_in, w_in_bf16)

    w_lora = jnp.zeros((2 * D_LORA, 2 * D_RWKV), F32)
    w_lora = w_lora.at[:D_LORA, :D_RWKV].set(w_lora_up[0]).at[D_LORA:, D_RWKV:].set(a_lora_up[0])
    vecs = jnp.concatenate([
        w0[0][None], a0[0][None], k_k[0][None], k_a[0][None], r_k[0].reshape(1, D_RWKV),
        lnx_g[0][None], lnx_b[0][None], conv_w[0],
        jnp.zeros((16 - 10, D_RWKV), F32)], axis=0)
    lane_head = jnp.arange(LANES) // HEAD
    seg_ones = (lane_head[:, None] == lane_head[None, :]).astype(BF16)
    t_idx = jnp.arange(CHUNK)
    tri = (t_idx[None, :] <= t_idx[:, None]).astype(BF16)
    tri2 = jnp.concatenate([tri, tri], axis=1)

    mix = _mixer(proj_meta, lora_meta, proj, lora, mu_shift, vecs, w_lora.astype(BF16), seg_ones, tri2)

    out = _outproj(mix.reshape(bsz * seq, D_MODEL), x2d, w_out[0].astype(BF16), norm_f_g[None, :])
    return out.reshape(bsz, seq, D_MODEL)
```

```python
import math

import jax
import jax.numpy as jnp
from jax import lax
from jax.experimental import pallas as pl
from jax.experimental.pallas import tpu as pltpu

F32 = jnp.float32
BF16 = jnp.bfloat16

D_MODEL = 2048
N_META = 16
D_RWKV = 1024
HEAD = 64
N_HEADS = D_RWKV // HEAD
D_LORA = 64
D_CONV = 1024
RW_COLS = 4 * D_RWKV + 2 * D_LORA
IN_COLS = RW_COLS + 4 * D_CONV
LORA_COLS = 2 * D_LORA
MAIN_COLS = IN_COLS - LORA_COLS
RMS_EPS = 1e-6
LNX_EPS = 64e-5

CHUNK = 64
LANES = 128
SUBLANES = 8
N_PAIRS = D_RWKV // LANES
FRONT_PAD = CHUNK - N_META

VMEM_LIMIT = 56 * 1024 * 1024
PROJ_VMEM_LIMIT = 60 * 1024 * 1024

INPROJ_TM = 1024
INPROJ_TN = 2048
META_TN = 1664
OUTPROJ_TM = 1024


def _rmsnorm_bf16(x, g):
    ms = jnp.mean(x * x, axis=-1, keepdims=True)
    return (x * lax.rsqrt(ms + RMS_EPS) * g).astype(BF16)


def _inproj_body(x_ref, g_ref, w_ref, wl_ref, o_ref, ol_ref, hn_ref):
    @pl.when(pl.program_id(2) == 0)
    def _():
        hn_ref[...] = _rmsnorm_bf16(x_ref[0], g_ref[...])
        ol_ref[0] = jnp.dot(hn_ref[...], wl_ref[...], preferred_element_type=F32)

    o_ref[0] = jnp.dot(hn_ref[...], w_ref[...], preferred_element_type=F32)


def _main_col_start(j):
    skip = jnp.where(j * INPROJ_TN >= 4 * D_RWKV, LORA_COLS // LANES, 0)
    return (j * (INPROJ_TN // LANES) + skip) * LANES


def _inproj(x, g, w_bf16):
    bsz, seq, _ = x.shape
    tm, tn = INPROJ_TM, INPROJ_TN
    assert seq % tm == 0 and MAIN_COLS % tn == 0 and (4 * D_RWKV) % tn == 0
    return pl.pallas_call(
        _inproj_body,
        grid=(bsz, seq // tm, MAIN_COLS // tn),
        in_specs=[
            pl.BlockSpec((1, tm, D_MODEL), lambda b, i, j: (b, i, 0)),
            pl.BlockSpec((1, D_MODEL), lambda b, i, j: (0, 0)),
            pl.BlockSpec((pl.Element(D_MODEL), pl.Element(tn)), lambda b, i, j: (0, _main_col_start(j))),
            pl.BlockSpec((D_MODEL, LORA_COLS), lambda b, i, j: (0, 4 * D_RWKV // LORA_COLS)),
        ],
        out_specs=[pl.BlockSpec((1, tm, tn), lambda b, i, j: (b, i, j)),
                   pl.BlockSpec((1, tm, LORA_COLS), lambda b, i, j: (b, i, 0))],
        out_shape=[jax.ShapeDtypeStruct((bsz, seq, MAIN_COLS), F32),
                   jax.ShapeDtypeStruct((bsz, seq, LORA_COLS), F32)],
        scratch_shapes=[pltpu.VMEM((tm, D_MODEL), BF16)],
        compiler_params=pltpu.CompilerParams(
            dimension_semantics=("arbitrary", "arbitrary", "arbitrary"),
            vmem_limit_bytes=PROJ_VMEM_LIMIT),
        name="inproj",
    )(x, g, w_bf16, w_bf16)


def _inproj_meta_body(x_ref, g_ref, w_ref, o_ref, wb_ref):
    w = w_ref[...].astype(BF16)
    wb_ref[...] = w
    o_ref[0] = jnp.dot(_rmsnorm_bf16(x_ref[...], g_ref[...]), w, preferred_element_type=F32)


def _inproj_meta(front, g, w_f32):
    tn = META_TN
    assert IN_COLS % tn == 0
    return pl.pallas_call(
        _inproj_meta_body,
        grid=(IN_COLS // tn,),
        in_specs=[
            pl.BlockSpec((CHUNK, D_MODEL), lambda j: (0, 0)),
            pl.BlockSpec((1, D_MODEL), lambda j: (0, 0)),
            pl.BlockSpec((D_MODEL, tn), lambda j: (0, j)),
        ],
        out_specs=[pl.BlockSpec((1, CHUNK, tn), lambda j: (0, 0, j)),
                   pl.BlockSpec((D_MODEL, tn), lambda j: (0, j))],
        out_shape=[jax.ShapeDtypeStruct((1, CHUNK, IN_COLS), F32),
                   jax.ShapeDtypeStruct((D_MODEL, IN_COLS), BF16)],
        compiler_params=pltpu.CompilerParams(
            dimension_semantics=("arbitrary",),
            vmem_limit_bytes=VMEM_LIMIT),
        name="inproj_meta",
    )(front, g, w_f32)


class _Split:
    def __init__(self, x):
        self.x = x
        self._hi = None
        self._lo = None

    @property
    def hi(self):
        if self._hi is None:
            self._hi = self.x.astype(BF16)
        return self._hi

    @property
    def lo(self):
        if self._lo is None:
            self._lo = (self.x - self.hi.astype(F32)).astype(BF16)
        return self._lo


class _BlockDiag:
    def __init__(self, y):
        self._y = y
        self._hi = None

    @property
    def hi(self):
        if self._hi is None:
            yp = self._y.hi
            lane = lax.broadcasted_iota(jnp.int32, yp.shape, 1)
            first = lane < HEAD
            zero = jnp.zeros_like(yp)
            self._hi = jnp.concatenate([jnp.where(first, yp, zero), jnp.where(first, zero, yp)], axis=0)
        return self._hi


def _mm(terms):
    la = jnp.concatenate([a.hi for a, _ in terms], axis=1)
    lb = jnp.concatenate([b.hi for _, b in terms], axis=0)
    return jnp.dot(la, lb, preferred_element_type=F32)


def _shift_rows(x, carry, n):
    row = lax.broadcasted_iota(jnp.int32, x.shape, 0)
    out = pltpu.roll(x, n, axis=0)
    for i in range(n):
        src = SUBLANES - n + i
        out = jnp.where(row == i, carry[src:src + 1, :], out)
    return out


def _sigmoid(x):
    return 0.5 + 0.5 * jnp.tanh(0.5 * x)


def _silu(x):
    h = 0.5 * x
    return h + h * jnp.tanh(h)


def _interleave(order, stages):
    for name in order:
        next(stages[name], None)
    for gen in stages.values():
        for _ in gen:
            pass


_HALF_ORDER = ("prep", "inverse", "prep", "inverse", "prep", "inverse", "inverse", "prep", "inverse", "advance",
               "inverse", "advance", "inverse", "advance", "finish", "inverse", "advance", "finish", "inverse",
               "finish", "inverse", "inverse", "prep")


def _mixer_body(pm_ref, lm_ref, px_ref, lx_ref, mu_ref, vec_ref, wl_ref, ones_ref, tri_ref, o_ref,
                carry_rw, carry_u, state, s_at, s_rt, s_kh, s_bh, s_v, s_pe, s_y,
                r_x, r_tak, r_ark, r_arb, e_bonus, e_gate, e_yb):
    def vec(i):
        return vec_ref[i:i + 1, :]

    w0, a0, k_k, k_a, r_k, lnx_g, lnx_b, cw0, cw1, cw2 = (vec(i) for i in range(10))
    ps = range(N_PAIRS)

    def seg_sums(xs):
        tiles = [x[:, p * LANES:(p + 1) * LANES] for x in xs for p in ps]
        t = jnp.concatenate(tiles, axis=0).astype(BF16)
        s = jnp.dot(t, ones_ref[...], preferred_element_type=F32)
        out = []
        for i in range(len(xs)):
            rows = [s[(i * N_PAIRS + p) * CHUNK:(i * N_PAIRS + p + 1) * CHUNK] for p in ps]
            out.append(jnp.concatenate(rows, axis=1))
        return out

    def prep(p_ref, l_ref, row0, sel):
        p_rw = jnp.concatenate([p_ref[0, row0:row0 + CHUNK, :4 * D_RWKV], l_ref[0, row0:row0 + CHUNK, :]],
                               axis=1)
        prev = _shift_rows(p_rw, carry_rw[...], 1)
        carry_rw[...] = p_rw[CHUNK - SUBLANES:, :]
        pm = p_rw + (prev - p_rw) * mu_ref[...]
        r = pm[:, 0 * D_RWKV:1 * D_RWKV]
        k = pm[:, 1 * D_RWKV:2 * D_RWKV]
        v = pm[:, 2 * D_RWKV:3 * D_RWKV]
        g_r = pm[:, 3 * D_RWKV:4 * D_RWKV]
        lora_in = pm[:, 4 * D_RWKV:]
        lane = lax.broadcasted_iota(jnp.int32, lora_in.shape, 1)
        lora_in = jnp.where(lane < D_LORA, jnp.tanh(lora_in), lora_in).astype(BF16)
        gate = _silu(g_r)

        p_cv = p_ref[0, row0:row0 + CHUNK, 4 * D_RWKV:]
        b_g = p_cv[:, 0 * D_CONV:1 * D_CONV]
        c_g = p_cv[:, 1 * D_CONV:2 * D_CONV]
        h_c = p_cv[:, 2 * D_CONV:3 * D_CONV]
        g_c = p_cv[:, 3 * D_CONV:4 * D_CONV]
        u_c = c_g * h_c
        carry = carry_u[...]
        conv = cw0 * _shift_rows(u_c, carry, 2) + cw1 * _shift_rows(u_c, carry, 1) + cw2 * u_c
        carry_u[...] = u_c[CHUNK - SUBLANES:, :]
        y_b = (b_g * conv * _silu(g_c)).astype(e_yb.dtype)
        yield
        lora = jnp.dot(lora_in, wl_ref[...], preferred_element_type=F32)
        logw = (-math.exp(-0.5)) * _sigmoid(w0 + lora[:, :D_RWKV])
        a = _sigmoid(a0 + lora[:, D_RWKV:])
        kk = k * k_k
        k = k * (1.0 + (a - 1.0) * k_a)
        yield
        logw = _Split(logw)
        cum = jnp.dot(tri_ref[...], jnp.concatenate([logw.hi, logw.lo], axis=0),
                      preferred_element_type=F32)
        logw = logw.x
        e_in = jnp.exp(cum)
        e_out = jnp.exp(-cum)
        e_ex = jnp.exp(cum - logw)
        p_end = e_in[CHUNK - 1:CHUNK, :]
        r_t = r * e_in
        k_h = k * e_out
        yield
        kk_sq, rk_sum = seg_sums([kk * kk, r * k * r_k])
        kk = kk * lax.rsqrt(jnp.maximum(kk_sq, 1e-24))
        a_t = kk * e_ex
        b_h = kk * a * e_out
        bonus = rk_sum * v
        outs = [val.astype(dst.dtype) for dst, val in ((s_at, a_t), (s_rt, r_t), (s_kh, k_h), (s_bh, b_h), (s_v, v))]
        yield
        for dst, val in zip((s_at, s_rt, s_kh, s_bh, s_v), outs):
            for p in ps:
                dst[sel, p] = val[:, p * LANES:(p + 1) * LANES]
        for p in ps:
            s_pe[sel, p] = jnp.broadcast_to(p_end[:, p * LANES:(p + 1) * LANES], (SUBLANES, LANES))
        e_bonus[sel] = bonus
        e_gate[sel] = gate
        e_yb[sel] = y_b

    t_i = lax.broadcasted_iota(jnp.int32, (CHUNK, LANES), 0)
    j_i = lax.broadcasted_iota(jnp.int32, (CHUNK, LANES), 1) & (HEAD - 1)

    def splits(xs):
        return [_Split(x) for x in xs]

    def pmm(xs, ys):
        return [_mm([(x, _BlockDiag(y))]) for x, y in zip(xs, ys)]

    def inverse(sel):
        strict = j_i < t_i
        incl = j_i <= t_i
        eye = (j_i == t_i).astype(F32)

        def same_block(log2_size):
            return (j_i >> log2_size) == (t_i >> log2_size)

        g = []
        for p in ps:
            lhs = jnp.concatenate([s_at[sel, p], s_rt[sel, p]], axis=0)
            rhs_t = jnp.concatenate([_BlockDiag(_Split(s_bh[sel, p])).hi,
                                     _BlockDiag(_Split(s_kh[sel, p])).hi], axis=0)
            g.append(lax.dot_general(lhs, rhs_t, (((1,), (1,)), ((), ())), preferred_element_type=F32))
        t_ab = [jnp.where(strict, g_[:CHUNK, :LANES], 0.0) for g_ in g]
        for p in ps:
            r_arb[sel, p] = jnp.where(incl, g[p][CHUNK:, :LANES], 0.0).astype(r_arb.dtype)
            r_tak[sel, p] = jnp.where(strict, g[p][:CHUNK, LANES:], 0.0).astype(r_tak.dtype)
            r_ark[sel, p] = jnp.where(incl, g[p][CHUNK:, LANES:], 0.0).astype(r_ark.dtype)
        yield

        x = splits([eye - jnp.where(same_block(1), t, 0.0) for t in t_ab])
        for log2_size in range(2, 7):
            level = same_block(log2_size) & jnp.logical_not(same_block(log2_size - 1))
            c = splits([jnp.where(level, t, 0.0) for t in t_ab])
            xc = splits(pmm(x, c))
            yield
            x = splits([x_.x - m for x_, m in zip(x, pmm(xc, x))])
            if log2_size < 6:
                yield
        for p in ps:
            r_x[sel, p] = x[p].hi

    def advance(sel):
        row = lax.broadcasted_iota(jnp.int32, (LANES, LANES), 0)
        col = lax.broadcasted_iota(jnp.int32, (LANES, LANES), 1)
        same_head = (row < HEAD) == (col < HEAD)
        vv = [s_v[sel, p] for p in ps]
        vbd = [_BlockDiag(_Split(v_)) for v_ in vv]
        s0 = [state[p] for p in ps]
        h0 = splits([s.T for s in s0])
        rhs = splits([_mm([(_Split(s_at[sel, p]), h0[p]), (_Split(r_tak[sel, p]), vbd[p])]) for p in ps])
        yield
        u = splits(pmm(splits([r_x[sel, p] for p in ps]), rhs))
        yield
        for p in ps:
            s_y[sel, p] = _mm([(_Split(s_rt[sel, p]), h0[p]), (_Split(r_ark[sel, p]), vbd[p]),
                               (_Split(-r_arb[sel, p]), _BlockDiag(u[p]))])
        yield
        for p in ps:
            wv_t = _Split(jnp.concatenate([vv[p], u[p].x], axis=0).T)
            upd = _mm([(wv_t, _Split(jnp.concatenate([s_kh[sel, p], -s_bh[sel, p]], axis=0)))])
            state[p] = jnp.where(same_head, (s0[p] + upd) * s_pe[sel, p][0:1, :], 0.0)

    def finish(sel, row0):
        y = jnp.concatenate([s_y[sel, p] for p in ps], axis=1)
        inv_n = 1.0 / HEAD
        yield
        mean = seg_sums([y])[0] * inv_n
        yc = y - mean
        yield
        var = seg_sums([yc * yc])[0] * inv_n
        yn = yc * lax.rsqrt(var + LNX_EPS) * lnx_g + lnx_b
        y_a = (yn + e_bonus[sel]) * e_gate[sel]
        o_ref[0, row0:row0 + CHUNK, :D_RWKV] = y_a.astype(o_ref.dtype)
        o_ref[0, row0:row0 + CHUNK, D_RWKV:] = e_yb[sel]

    @pl.when(pl.program_id(1) == 0)
    def _():
        for ref in (carry_rw, carry_u, state, s_at, s_rt, s_kh, s_bh, s_v, s_pe,
                    r_x, r_tak, r_ark, r_arb, e_bonus, e_gate, e_yb):
            ref[...] = jnp.zeros_like(ref)
        for _ in prep(pm_ref, lm_ref, 0, 0):
            pass

    _interleave(_HALF_ORDER, {"advance": advance(1), "finish": finish(1, 0), "inverse": inverse(0),
                              "prep": prep(px_ref, lx_ref, 0, 1)})
    _interleave(_HALF_ORDER, {"advance": advance(0), "finish": finish(0, CHUNK), "inverse": inverse(1),
                              "prep": prep(px_ref, lx_ref, CHUNK, 0)})


def _mixer(proj_meta, lora_meta, proj, lora, mu, vecs, w_lora, seg_ones, tri2):
    bsz, seq, _ = proj.shape
    assert seq % (2 * CHUNK) == 0
    n_steps = seq // (2 * CHUNK) + 1
    pair_buf = pltpu.VMEM((2, N_PAIRS, CHUNK, LANES), F32)
    pair_bf16 = pltpu.VMEM((2, N_PAIRS, CHUNK, LANES), BF16)
    row_buf = pltpu.VMEM((2, CHUNK, D_RWKV), F32)
    const = lambda shape: pl.BlockSpec(shape, lambda b, j: (0,) * len(shape))
    return pl.pallas_call(
        _mixer_body,
        grid=(bsz, n_steps),
        in_specs=[
            const(proj_meta.shape), const(lora_meta.shape),
            pl.BlockSpec((1, 2 * CHUNK, MAIN_COLS), lambda b, j: (b, jnp.minimum(j, n_steps - 2), 0)),
            pl.BlockSpec((1, 2 * CHUNK, LORA_COLS), lambda b, j: (b, jnp.minimum(j, n_steps - 2), 0)),
            const(mu.shape), const(vecs.shape), const(w_lora.shape),
            const(seg_ones.shape), const(tri2.shape),
        ],
        out_specs=pl.BlockSpec((1, 2 * CHUNK, D_MODEL), lambda b, j: (b, jnp.maximum(j - 1, 0), 0)),
        out_shape=jax.ShapeDtypeStruct((bsz, seq, D_MODEL), BF16),
        scratch_shapes=[
            pltpu.VMEM((SUBLANES, RW_COLS), F32),
            pltpu.VMEM((SUBLANES, D_CONV), F32),
            pltpu.VMEM((N_PAIRS, LANES, LANES), F32),
            pair_bf16, pair_bf16, pair_bf16, pair_bf16, pair_buf,
            pltpu.VMEM((2, N_PAIRS, SUBLANES, LANES), F32),
            pair_buf,
            pair_bf16, pair_bf16, pair_bf16, pair_bf16,
            row_buf, row_buf,
            pltpu.VMEM((2, CHUNK, D_CONV), BF16),
        ],
        compiler_params=pltpu.CompilerParams(
            dimension_semantics=("arbitrary", "arbitrary"),
            vmem_limit_bytes=VMEM_LIMIT),
        name="mixer",
    )(proj_meta, lora_meta, proj, lora, mu, vecs, w_lora, seg_ones, tri2)


def _outproj_body(mix_ref, x_ref, w_ref, g_ref, o_ref):
    h = x_ref[...] + jnp.dot(mix_ref[...], w_ref[...], preferred_element_type=F32)
    ms = jnp.mean(h * h, axis=-1, keepdims=True)
    o_ref[...] = h * lax.rsqrt(ms + RMS_EPS) * g_ref[...]


def _outproj(mix2d, x2d, w_bf16, g):
    rows = x2d.shape[0]
    tm = OUTPROJ_TM
    assert rows % tm == 0
    return pl.pallas_call(
        _outproj_body,
        grid=(rows // tm,),
        in_specs=[
            pl.BlockSpec((tm, D_MODEL), lambda i: (i, 0)),
            pl.BlockSpec((tm, D_MODEL), lambda i: (i, 0)),
            pl.BlockSpec((D_MODEL, D_MODEL), lambda i: (0, 0), pipeline_mode=pl.Buffered(1)),
            pl.BlockSpec((1, D_MODEL), lambda i: (0, 0)),
        ],
        out_specs=pl.BlockSpec((tm, D_MODEL), lambda i: (i, 0)),
        out_shape=jax.ShapeDtypeStruct((rows, D_MODEL), F32),
        compiler_params=pltpu.CompilerParams(
            dimension_semantics=("arbitrary",),
            vmem_limit_bytes=PROJ_VMEM_LIMIT),
        name="outproj",
    )(mix2d, x2d, w_bf16, g)


def kernel(x, meta_tokens, norm_in_g, w_in, mu_shift, w0, w_lora_up, a0, a_lora_up, k_k, k_a, r_k, lnx_g, lnx_b, conv_w, w_out, norm_f_g):
    bsz, seq, _ = x.shape
    assert norm_in_g.shape[0] == 1 and seq % CHUNK == 0
    x2d = x.reshape(bsz * seq, D_MODEL)

    front = jnp.concatenate([jnp.zeros((FRONT_PAD, D_MODEL), x.dtype), meta_tokens.astype(x.dtype)], axis=0)
    g_in = norm_in_g[0][None, :]
    meta_all, w_in_bf16 = _inproj_meta(front, g_in, w_in[0])
    proj_meta = jnp.concatenate([meta_all[..., :4 * D_RWKV], meta_all[..., RW_COLS:]], axis=-1)
    lora_meta = meta_all[..., 4 * D_RWKV:RW_COLS]
    proj, lora = _inproj(x, g_in, w_in_bf16)

    w_lora = jnp.zeros((2 * D_LORA, 2 * D_RWKV), F32)
    w_lora = w_lora.at[:D_LORA, :D_RWKV].set(w_lora_up[0]).at[D_LORA:, D_RWKV:].set(a_lora_up[0])
    vecs = jnp.concatenate([
        w0[0][None], a0[0][None], k_k[0][None], k_a[0][None], r_k[0].reshape(1, D_RWKV),
        lnx_g[0][None], lnx_b[0][None], conv_w[0],
        jnp.zeros((16 - 10, D_RWKV), F32)], axis=0)
    lane_head = jnp.arange(LANES) // HEAD
    seg_ones = (lane_head[:, None] == lane_head[None, :]).astype(BF16)
    t_idx = jnp.arange(CHUNK)
    tri = (t_idx[None, :] <= t_idx[:, None]).astype(BF16)
    tri2 = jnp.concatenate([tri, tri], axis=1)

    mix = _mixer(proj_meta, lora_meta, proj, lora, mu_shift, vecs, w_lora.astype(BF16), seg_ones, tri2)

    out = _outproj(mix.reshape(bsz * seq, D_MODEL), x2d, w_out[0].astype(BF16), norm_f_g[None, :])
    return out.reshape(bsz, seq, D_MODEL)
```

```python
import math

import jax
import jax.numpy as jnp
from jax import lax
from jax.experimental import pallas as pl
from jax.experimental.pallas import tpu as pltpu

F32 = jnp.float32
BF16 = jnp.bfloat16

D_MODEL = 2048
N_META = 16
D_RWKV = 1024
HEAD = 64
N_HEADS = D_RWKV // HEAD
D_LORA = 64
D_CONV = 1024
RW_COLS = 4 * D_RWKV + 2 * D_LORA
IN_COLS = RW_COLS + 4 * D_CONV
LORA_COLS = 2 * D_LORA
MAIN_COLS = IN_COLS - LORA_COLS
RMS_EPS = 1e-6
LNX_EPS = 64e-5

CHUNK = 64
LANES = 128
SUBLANES = 8
N_PAIRS = D_RWKV // LANES
FRONT_PAD = CHUNK - N_META

VMEM_LIMIT = 56 * 1024 * 1024
PROJ_VMEM_LIMIT = 60 * 1024 * 1024

INPROJ_TM = 1024
INPROJ_TN = 2048
META_TN = 1664
OUTPROJ_TM = 1024


def _rmsnorm_bf16(x, g):
    ms = jnp.mean(x * x, axis=-1, keepdims=True)
    return (x * lax.rsqrt(ms + RMS_EPS) * g).astype(BF16)


def _inproj_body(x_ref, g_ref, w_ref, wl_ref, o_ref, ol_ref, hn_ref):
    @pl.when(pl.program_id(2) == 0)
    def _():
        hn_ref[...] = _rmsnorm_bf16(x_ref[0], g_ref[...])
        ol_ref[0] = jnp.dot(hn_ref[...], wl_ref[...], preferred_element_type=F32)

    o_ref[0] = jnp.dot(hn_ref[...], w_ref[...], preferred_element_type=F32)


def _main_col_start(j):
    skip = jnp.where(j * INPROJ_TN >= 4 * D_RWKV, LORA_COLS // LANES, 0)
    return (j * (INPROJ_TN // LANES) + skip) * LANES


def _inproj(x, g, w_bf16):
    bsz, seq, _ = x.shape
    tm, tn = INPROJ_TM, INPROJ_TN
    assert seq % tm == 0 and MAIN_COLS % tn == 0 and (4 * D_RWKV) % tn == 0
    return pl.pallas_call(
        _inproj_body,
        grid=(bsz, seq // tm, MAIN_COLS // tn),
        in_specs=[
            pl.BlockSpec((1, tm, D_MODEL), lambda b, i, j: (b, i, 0)),
            pl.BlockSpec((1, D_MODEL), lambda b, i, j: (0, 0)),
            pl.BlockSpec((pl.Element(D_MODEL), pl.Element(tn)), lambda b, i, j: (0, _main_col_start(j))),
            pl.BlockSpec((D_MODEL, LORA_COLS), lambda b, i, j: (0, 4 * D_RWKV // LORA_COLS)),
        ],
        out_specs=[pl.BlockSpec((1, tm, tn), lambda b, i, j: (b, i, j)),
                   pl.BlockSpec((1, tm, LORA_COLS), lambda b, i, j: (b, i, 0))],
        out_shape=[jax.ShapeDtypeStruct((bsz, seq, MAIN_COLS), F32),
                   jax.ShapeDtypeStruct((bsz, seq, LORA_COLS), F32)],
        scratch_shapes=[pltpu.VMEM((tm, D_MODEL), BF16)],
        compiler_params=pltpu.CompilerParams(
            dimension_semantics=("arbitrary", "arbitrary", "arbitrary"),
            vmem_limit_bytes=PROJ_VMEM_LIMIT),
        name="inproj",
    )(x, g, w_bf16, w_bf16)


def _inproj_meta_body(x_ref, g_ref, w_ref, o_ref, wb_ref):
    w = w_ref[...].astype(BF16)
    wb_ref[...] = w
    o_ref[0] = jnp.dot(_rmsnorm_bf16(x_ref[...], g_ref[...]), w, preferred_element_type=F32)


def _inproj_meta(front, g, w_f32):
    tn = META_TN
    assert IN_COLS % tn == 0
    return pl.pallas_call(
        _inproj_meta_body,
        grid=(IN_COLS // tn,),
        in_specs=[
            pl.BlockSpec((CHUNK, D_MODEL), lambda j: (0, 0)),
            pl.BlockSpec((1, D_MODEL), lambda j: (0, 0)),
            pl.BlockSpec((D_MODEL, tn), lambda j: (0, j)),
        ],
        out_specs=[pl.BlockSpec((1, CHUNK, tn), lambda j: (0, 0, j)),
                   pl.BlockSpec((D_MODEL, tn), lambda j: (0, j))],
        out_shape=[jax.ShapeDtypeStruct((1, CHUNK, IN_COLS), F32),
                   jax.ShapeDtypeStruct((D_MODEL, IN_COLS), BF16)],
        compiler_params=pltpu.CompilerParams(
            dimension_semantics=("arbitrary",),
            vmem_limit_bytes=VMEM_LIMIT),
        name="inproj_meta",
    )(front, g, w_f32)


class _Split:
    def __init__(self, x):
        self.x = x
        self._hi = None
        self._lo = None

    @property
    def hi(self):
        if self._hi is None:
            self._hi = self.x.astype(BF16)
        return self._hi

    @property
    def lo(self):
        if self._lo is None:
            self._lo = (self.x - self.hi.astype(F32)).astype(BF16)
        return self._lo


class _BlockDiag:
    def __init__(self, y):
        self._y = y
        self._hi = None

    @property
    def hi(self):
        if self._hi is None:
            yp = self._y.hi
            lane = lax.broadcasted_iota(jnp.int32, yp.shape, 1)
            first = lane < HEAD
            zero = jnp.zeros_like(yp)
            self._hi = jnp.concatenate([jnp.where(first, yp, zero), jnp.where(first, zero, yp)], axis=0)
        return self._hi


def _mm(terms):
    la = jnp.concatenate([a.hi for a, _ in terms], axis=1)
    lb = jnp.concatenate([b.hi for _, b in terms], axis=0)
    return jnp.dot(la, lb, preferred_element_type=F32)


def _shift_rows(x, carry, n):
    row = lax.broadcasted_iota(jnp.int32, x.shape, 0)
    out = pltpu.roll(x, n, axis=0)
    for i in range(n):
        src = SUBLANES - n + i
        out = jnp.where(row == i, carry[src:src + 1, :], out)
    return out


def _sigmoid(x):
    return 0.5 + 0.5 * jnp.tanh(0.5 * x)


def _silu(x):
    h = 0.5 * x
    return h + h * jnp.tanh(h)


def _interleave(order, stages):
    for name in order:
        if name in stages:
            next(stages[name], None)
    for gen in stages.values():
        for _ in gen:
            pass


_HALF_ORDER = ("prep", "inverse", "prep", "inverse", "prep", "inverse", "inverse", "prep", "inverse", "advance",
               "inverse", "advance", "inverse", "advance", "finish", "inverse", "advance", "finish", "inverse",
               "finish", "inverse", "inverse", "prep")


def _mixer_body(pm_ref, lm_ref, px_ref, lx_ref, mu_ref, vec_ref, wl_ref, ones_ref, tri_ref, o_ref,
                carry_rw, carry_u, state, s_at, s_rt, s_kh, s_bh, s_v, s_pe, s_y,
                r_x, r_tak, r_ark, r_arb, e_bonus, e_gate, e_yb):
    def vec(i):
        return vec_ref[i:i + 1, :]

    w0, a0, k_k, k_a, r_k, lnx_g, lnx_b, cw0, cw1, cw2 = (vec(i) for i in range(10))
    ps = range(N_PAIRS)

    def seg_sums(xs):
        tiles = [x[:, p * LANES:(p + 1) * LANES] for x in xs for p in ps]
        t = jnp.concatenate(tiles, axis=0).astype(BF16)
        s = jnp.dot(t, ones_ref[...], preferred_element_type=F32)
        out = []
        for i in range(len(xs)):
            rows = [s[(i * N_PAIRS + p) * CHUNK:(i * N_PAIRS + p + 1) * CHUNK] for p in ps]
            out.append(jnp.concatenate(rows, axis=1))
        return out

    def prep(p_ref, l_ref, row0, sel):
        p_rw = jnp.concatenate([p_ref[0, row0:row0 + CHUNK, :4 * D_RWKV], l_ref[0, row0:row0 + CHUNK, :]],
                               axis=1)
        prev = _shift_rows(p_rw, carry_rw[...], 1)
        carry_rw[...] = p_rw[CHUNK - SUBLANES:, :]
        pm = p_rw + (prev - p_rw) * mu_ref[...]
        r = pm[:, 0 * D_RWKV:1 * D_RWKV]
        k = pm[:, 1 * D_RWKV:2 * D_RWKV]
        v = pm[:, 2 * D_RWKV:3 * D_RWKV]
        g_r = pm[:, 3 * D_RWKV:4 * D_RWKV]
        lora_in = pm[:, 4 * D_RWKV:]
        lane = lax.broadcasted_iota(jnp.int32, lora_in.shape, 1)
        lora_in = jnp.where(lane < D_LORA, jnp.tanh(lora_in), lora_in).astype(BF16)
        gate = _silu(g_r)

        p_cv = p_ref[0, row0:row0 + CHUNK, 4 * D_RWKV:]
        b_g = p_cv[:, 0 * D_CONV:1 * D_CONV]
        c_g = p_cv[:, 1 * D_CONV:2 * D_CONV]
        h_c = p_cv[:, 2 * D_CONV:3 * D_CONV]
        g_c = p_cv[:, 3 * D_CONV:4 * D_CONV]
        u_c = c_g * h_c
        carry = carry_u[...]
        conv = cw0 * _shift_rows(u_c, carry, 2) + cw1 * _shift_rows(u_c, carry, 1) + cw2 * u_c
        carry_u[...] = u_c[CHUNK - SUBLANES:, :]
        y_b = (b_g * conv * _silu(g_c)).astype(e_yb.dtype)
        yield
        lora = jnp.dot(lora_in, wl_ref[...], preferred_element_type=F32)
        logw = (-math.exp(-0.5)) * _sigmoid(w0 + lora[:, :D_RWKV])
        a = _sigmoid(a0 + lora[:, D_RWKV:])
        kk = k * k_k
        k = k * (1.0 + (a - 1.0) * k_a)
        yield
        logw = _Split(logw)
        cum = jnp.dot(tri_ref[...], jnp.concatenate([logw.hi, logw.lo], axis=0),
                      preferred_element_type=F32)
        logw = logw.x
        e_in = jnp.exp(cum)
        e_out = jnp.exp(-cum)
        e_ex = jnp.exp(cum - logw)
        p_end = e_in[CHUNK - 1:CHUNK, :]
        r_t = r * e_in
        k_h = k * e_out
        yield
        kk_sq, rk_sum = seg_sums([kk * kk, r * k * r_k])
        kk = kk * lax.rsqrt(jnp.maximum(kk_sq, 1e-24))
        a_t = kk * e_ex
        b_h = kk * a * e_out
        bonus = rk_sum * v
        outs = [val.astype(dst.dtype) for dst, val in ((s_at, a_t), (s_rt, r_t), (s_kh, k_h), (s_bh, b_h), (s_v, v))]
        yield
        for dst, val in zip((s_at, s_rt, s_kh, s_bh, s_v), outs):
            for p in ps:
                dst[sel, p] = val[:, p * LANES:(p + 1) * LANES]
        for p in ps:
            s_pe[sel, p] = jnp.broadcast_to(p_end[:, p * LANES:(p + 1) * LANES], (SUBLANES, LANES))
        e_bonus[sel] = bonus
        e_gate[sel] = gate
        e_yb[sel] = y_b

    t_i = lax.broadcasted_iota(jnp.int32, (CHUNK, LANES), 0)
    j_i = lax.broadcasted_iota(jnp.int32, (CHUNK, LANES), 1) & (HEAD - 1)

    def splits(xs):
        return [_Split(x) for x in xs]

    def pmm(xs, ys):
        return [_mm([(x, _BlockDiag(y))]) for x, y in zip(xs, ys)]

    def inverse(sel):
        strict = j_i < t_i
        incl = j_i <= t_i
        eye = (j_i == t_i).astype(F32)

        def same_block(log2_size):
            return (j_i >> log2_size) == (t_i >> log2_size)

        g = []
        for p in ps:
            lhs = jnp.concatenate([s_at[sel, p], s_rt[sel, p]], axis=0)
            rhs_t = jnp.concatenate([_BlockDiag(_Split(s_bh[sel, p])).hi,
                                     _BlockDiag(_Split(s_kh[sel, p])).hi], axis=0)
            g.append(lax.dot_general(lhs, rhs_t, (((1,), (1,)), ((), ())), preferred_element_type=F32))
        t_ab = [jnp.where(strict, g_[:CHUNK, :LANES], 0.0) for g_ in g]
        for p in ps:
            r_arb[sel, p] = jnp.where(incl, g[p][CHUNK:, :LANES], 0.0).astype(r_arb.dtype)
            r_tak[sel, p] = jnp.where(strict, g[p][:CHUNK, LANES:], 0.0).astype(r_tak.dtype)
            r_ark[sel, p] = jnp.where(incl, g[p][CHUNK:, LANES:], 0.0).astype(r_ark.dtype)
        yield

        x = splits([eye - jnp.where(same_block(1), t, 0.0) for t in t_ab])
        for log2_size in range(2, 7):
            level = same_block(log2_size) & jnp.logical_not(same_block(log2_size - 1))
            c = splits([jnp.where(level, t, 0.0) for t in t_ab])
            xc = splits(pmm(x, c))
            yield
            x = splits([x_.x - m for x_, m in zip(x, pmm(xc, x))])
            if log2_size < 6:
                yield
        for p in ps:
            r_x[sel, p] = x[p].hi

    def advance(sel):
        row = lax.broadcasted_iota(jnp.int32, (LANES, LANES), 0)
        col = lax.broadcasted_iota(jnp.int32, (LANES, LANES), 1)
        same_head = (row < HEAD) == (col < HEAD)
        vv = [s_v[sel, p] for p in ps]
        vbd = [_BlockDiag(_Split(v_)) for v_ in vv]
        s0 = [state[p] for p in ps]
        h0 = splits([s.T for s in s0])
        rhs = splits([_mm([(_Split(s_at[sel, p]), h0[p]), (_Split(r_tak[sel, p]), vbd[p])]) for p in ps])
        yield
        u = splits(pmm(splits([r_x[sel, p] for p in ps]), rhs))
        yield
        for p in ps:
            s_y[sel, p] = _mm([(_Split(s_rt[sel, p]), h0[p]), (_Split(r_ark[sel, p]), vbd[p]),
                               (_Split(-r_arb[sel, p]), _BlockDiag(u[p]))])
        yield
        for p in ps:
            wv_t = _Split(jnp.concatenate([vv[p], u[p].x], axis=0).T)
            upd = _mm([(wv_t, _Split(jnp.concatenate([s_kh[sel, p], -s_bh[sel, p]], axis=0)))])
            state[p] = jnp.where(same_head, (s0[p] + upd) * s_pe[sel, p][0:1, :], 0.0)

    def finish(sel, row0):
        y = jnp.concatenate([s_y[sel, p] for p in ps], axis=1)
        inv_n = 1.0 / HEAD
        yield
        mean = seg_sums([y])[0] * inv_n
        yc = y - mean
        yield
        var = seg_sums([yc * yc])[0] * inv_n
        yn = yc * lax.rsqrt(var + LNX_EPS) * lnx_g + lnx_b
        y_a = (yn + e_bonus[sel]) * e_gate[sel]
        o_ref[0, row0:row0 + CHUNK, :D_RWKV] = y_a.astype(o_ref.dtype)
        o_ref[0, row0:row0 + CHUNK, D_RWKV:] = e_yb[sel]

    @pl.when(pl.program_id(1) == 0)
    def _():
        for ref in (carry_rw, carry_u, state, s_at, s_rt, s_kh, s_bh, s_v, s_pe,
                    r_x, r_tak, r_ark, r_arb, e_bonus, e_gate, e_yb):
            ref[...] = jnp.zeros_like(ref)
        for _ in prep(pm_ref, lm_ref, 0, 0):
            pass

    is_drain = pl.program_id(1) == pl.num_programs(1) - 1

    @pl.when(jnp.logical_not(is_drain))
    def _():
        _interleave(_HALF_ORDER, {"advance": advance(1), "finish": finish(1, 0), "inverse": inverse(0),
                                  "prep": prep(px_ref, lx_ref, 0, 1)})
        _interleave(_HALF_ORDER, {"advance": advance(0), "finish": finish(0, CHUNK), "inverse": inverse(1),
                                  "prep": prep(px_ref, lx_ref, CHUNK, 0)})

    @pl.when(is_drain)
    def _():
        _interleave(_HALF_ORDER, {"advance": advance(1), "finish": finish(1, 0), "inverse": inverse(0)})
        _interleave(_HALF_ORDER, {"advance": advance(0), "finish": finish(0, CHUNK)})


def _mixer(proj_meta, lora_meta, proj, lora, mu, vecs, w_lora, seg_ones, tri2):
    bsz, seq, _ = proj.shape
    assert seq % (2 * CHUNK) == 0
    n_steps = seq // (2 * CHUNK) + 1
    pair_buf = pltpu.VMEM((2, N_PAIRS, CHUNK, LANES), F32)
    pair_bf16 = pltpu.VMEM((2, N_PAIRS, CHUNK, LANES), BF16)
    row_buf = pltpu.VMEM((2, CHUNK, D_RWKV), F32)
    const = lambda shape: pl.BlockSpec(shape, lambda b, j: (0,) * len(shape))
    return pl.pallas_call(
        _mixer_body,
        grid=(bsz, n_steps),
        in_specs=[
            const(proj_meta.shape), const(lora_meta.shape),
            pl.BlockSpec((1, 2 * CHUNK, MAIN_COLS), lambda b, j: (b, jnp.minimum(j, n_steps - 2), 0)),
            pl.BlockSpec((1, 2 * CHUNK, LORA_COLS), lambda b, j: (b, jnp.minimum(j, n_steps - 2), 0)),
            const(mu.shape), const(vecs.shape), const(w_lora.shape),
            const(seg_ones.shape), const(tri2.shape),
        ],
        out_specs=pl.BlockSpec((1, 2 * CHUNK, D_MODEL), lambda b, j: (b, jnp.maximum(j - 1, 0), 0)),
        out_shape=jax.ShapeDtypeStruct((bsz, seq, D_MODEL), BF16),
        scratch_shapes=[
            pltpu.VMEM((SUBLANES, RW_COLS), F32),
            pltpu.VMEM((SUBLANES, D_CONV), F32),
            pltpu.VMEM((N_PAIRS, LANES, LANES), F32),
            pair_bf16, pair_bf16, pair_bf16, pair_bf16, pair_buf,
            pltpu.VMEM((2, N_PAIRS, SUBLANES, LANES), F32),
            pair_buf,
            pair_bf16, pair_bf16, pair_bf16, pair_bf16,
            row_buf, row_buf,
            pltpu.VMEM((2, CHUNK, D_CONV), BF16),
        ],
        compiler_params=pltpu.CompilerParams(
            dimension_semantics=("arbitrary", "arbitrary"),
            vmem_limit_bytes=VMEM_LIMIT),
        name="mixer",
    )(proj_meta, lora_meta, proj, lora, mu, vecs, w_lora, seg_ones, tri2)


def _outproj_body(mix_ref, x_ref, w_ref, g_ref, o_ref):
    h = x_ref[...] + jnp.dot(mix_ref[...], w_ref[...], preferred_element_type=F32)
    ms = jnp.mean(h * h, axis=-1, keepdims=True)
    o_ref[...] = h * lax.rsqrt(ms + RMS_EPS) * g_ref[...]


def _outproj(mix2d, x2d, w_bf16, g):
    rows = x2d.shape[0]
    tm = OUTPROJ_TM
    assert rows % tm == 0
    return pl.pallas_call(
        _outproj_body,
        grid=(rows // tm,),
        in_specs=[
            pl.BlockSpec((tm, D_MODEL), lambda i: (i, 0)),
            pl.BlockSpec((tm, D_MODEL), lambda i: (i, 0)),
            pl.BlockSpec((D_MODEL, D_MODEL), lambda i: (0, 0), pipeline_mode=pl.Buffered(1)),
            pl.BlockSpec((1, D_MODEL), lambda i: (0, 0)),
        ],
        out_specs=pl.BlockSpec((tm, D_MODEL), lambda i: (i, 0)),
        out_shape=jax.ShapeDtypeStruct((rows, D_MODEL), F32),
        compiler_params=pltpu.CompilerParams(
            dimension_semantics=("arbitrary",),
            vmem_limit_bytes=PROJ_VMEM_LIMIT),
        name="outproj",
    )(mix2d, x2d, w_bf16, g)


def kernel(x, meta_tokens, norm_in_g, w_in, mu_shift, w0, w_lora_up, a0, a_lora_up, k_k, k_a, r_k, lnx_g, lnx_b, conv_w, w_out, norm_f_g):
    bsz, seq, _ = x.shape
    assert norm_in_g.shape[0] == 1 and seq % CHUNK == 0
    x2d = x.reshape(bsz * seq, D_MODEL)

    front = jnp.concatenate([jnp.zeros((FRONT_PAD, D_MODEL), x.dtype), meta_tokens.astype(x.dtype)], axis=0)
    g_in = norm_in_g[0][None, :]
    meta_all, w_in_bf16 = _inproj_meta(front, g_in, w_in[0])
    proj_meta = jnp.concatenate([meta_all[..., :4 * D_RWKV], meta_all[..., RW_COLS:]], axis=-1)
    lora_meta = meta_all[..., 4 * D_RWKV:RW_COLS]
    proj, lora = _inproj(x, g_in, w_in_bf16)

    w_lora = jnp.zeros((2 * D_LORA, 2 * D_RWKV), F32)
    w_lora = w_lora.at[:D_LORA, :D_RWKV].set(w_lora_up[0]).at[D_LORA:, D_RWKV:].set(a_lora_up[0])
    vecs = jnp.concatenate([
        w0[0][None], a0[0][None], k_k[0][None], k_a[0][None], r_k[0].reshape(1, D_RWKV),
        lnx_g[0][None], lnx_b[0][None], conv_w[0],
        jnp.zeros((16 - 10, D_RWKV), F32)], axis=0)
    lane_head = jnp.arange(LANES) // HEAD
    seg_ones = (lane_head[:, None] == lane_head[None, :]).astype(BF16)
    t_idx = jnp.arange(CHUNK)
    tri = (t_idx[None, :] <= t_idx[:, None]).astype(BF16)
    tri2 = jnp.concatenate([tri, tri], axis=1)

    mix = _mixer(proj_meta, lora_meta, proj, lora, mu_shift, vecs, w_lora.astype(BF16), seg_ones, tri2)

    out = _outproj(mix.reshape(bsz * seq, D_MODEL), x2d, w_out[0].astype(BF16), norm_f_g[None, :])
    return out.reshape(bsz, seq, D_MODEL)
```

```python
import math

import jax
import jax.numpy as jnp
from jax import lax
from jax.experimental import pallas as pl
from jax.experimental.pallas import tpu as pltpu

F32 = jnp.float32
BF16 = jnp.bfloat16

D_MODEL = 2048
N_META = 16
D_RWKV = 1024
HEAD = 64
N_HEADS = D_RWKV // HEAD
D_LORA = 64
D_CONV = 1024
RW_COLS = 4 * D_RWKV + 2 * D_LORA
IN_COLS = RW_COLS + 4 * D_CONV
LORA_COLS = 2 * D_LORA
MAIN_COLS = IN_COLS - LORA_COLS
RW_PAD = 8192
RMS_EPS = 1e-6
LNX_EPS = 64e-5

CHUNK = 64
LANES = 128
SUBLANES = 8
N_PAIRS = D_RWKV // LANES
FRONT_PAD = CHUNK - N_META

VMEM_LIMIT = 56 * 1024 * 1024
PROJ_VMEM_LIMIT = 60 * 1024 * 1024

INPROJ_TM = 1024
INPROJ_TN = 2048
META_TN = 1664
OUTPROJ_TM = 1024


def _rmsnorm_bf16(x, g):
    ms = jnp.mean(x * x, axis=-1, keepdims=True)
    return (x * lax.rsqrt(ms + RMS_EPS) * g).astype(BF16)


def _inproj_body(x_ref, g_ref, w_ref, wl_ref, o_ref, ol_ref, hn_ref):
    @pl.when(pl.program_id(2) == 0)
    def _():
        hn_ref[...] = _rmsnorm_bf16(x_ref[0], g_ref[...])
        ol_ref[0] = jnp.dot(hn_ref[...], wl_ref[...], preferred_element_type=F32)

    o_ref[0] = jnp.dot(hn_ref[...], w_ref[...], preferred_element_type=F32)


def _main_col_start(j):
    skip = jnp.where(j * INPROJ_TN >= 4 * D_RWKV, LORA_COLS // LANES, 0)
    return (j * (INPROJ_TN // LANES) + skip) * LANES


def _inproj(x, g, w_bf16):
    bsz, seq, _ = x.shape
    tm, tn = INPROJ_TM, INPROJ_TN
    assert seq % tm == 0 and MAIN_COLS % tn == 0 and (4 * D_RWKV) % tn == 0
    return pl.pallas_call(
        _inproj_body,
        grid=(bsz, seq // tm, MAIN_COLS // tn),
        in_specs=[
            pl.BlockSpec((1, tm, D_MODEL), lambda b, i, j: (b, i, 0)),
            pl.BlockSpec((1, D_MODEL), lambda b, i, j: (0, 0)),
            pl.BlockSpec((pl.Element(D_MODEL), pl.Element(tn)), lambda b, i, j: (0, _main_col_start(j))),
            pl.BlockSpec((D_MODEL, LORA_COLS), lambda b, i, j: (0, 4 * D_RWKV // LORA_COLS)),
        ],
        out_specs=[pl.BlockSpec((1, tm, tn), lambda b, i, j: (b, i, j)),
                   pl.BlockSpec((1, tm, LORA_COLS), lambda b, i, j: (b, i, 0))],
        out_shape=[jax.ShapeDtypeStruct((bsz, seq, MAIN_COLS), F32),
                   jax.ShapeDtypeStruct((bsz, seq, LORA_COLS), F32)],
        scratch_shapes=[pltpu.VMEM((tm, D_MODEL), BF16)],
        compiler_params=pltpu.CompilerParams(
            dimension_semantics=("arbitrary", "arbitrary", "arbitrary"),
            vmem_limit_bytes=PROJ_VMEM_LIMIT),
        name="inproj",
    )(x, g, w_bf16, w_bf16)


def _inproj_meta_body(x_ref, g_ref, w_ref, o_ref, wb_ref):
    w = w_ref[...].astype(BF16)
    wb_ref[...] = w
    o_ref[0] = jnp.dot(_rmsnorm_bf16(x_ref[...], g_ref[...]), w, preferred_element_type=F32)


def _inproj_meta(front, g, w_f32):
    tn = META_TN
    assert IN_COLS % tn == 0
    return pl.pallas_call(
        _inproj_meta_body,
        grid=(IN_COLS // tn,),
        in_specs=[
            pl.BlockSpec((CHUNK, D_MODEL), lambda j: (0, 0)),
            pl.BlockSpec((1, D_MODEL), lambda j: (0, 0)),
            pl.BlockSpec((D_MODEL, tn), lambda j: (0, j)),
        ],
        out_specs=[pl.BlockSpec((1, CHUNK, tn), lambda j: (0, 0, j)),
                   pl.BlockSpec((D_MODEL, tn), lambda j: (0, j))],
        out_shape=[jax.ShapeDtypeStruct((1, CHUNK, IN_COLS), F32),
                   jax.ShapeDtypeStruct((D_MODEL, IN_COLS), BF16)],
        compiler_params=pltpu.CompilerParams(
            dimension_semantics=("arbitrary",),
            vmem_limit_bytes=VMEM_LIMIT),
        name="inproj_meta",
    )(front, g, w_f32)


class _Split:
    def __init__(self, x):
        self.x = x
        self._hi = None
        self._lo = None

    @property
    def hi(self):
        if self._hi is None:
            self._hi = self.x.astype(BF16)
        return self._hi

    @property
    def lo(self):
        if self._lo is None:
            self._lo = (self.x - self.hi.astype(F32)).astype(BF16)
        return self._lo


class _BlockDiag:
    def __init__(self, y):
        self._y = y
        self._hi = None

    @property
    def hi(self):
        if self._hi is None:
            yp = self._y.hi
            lane = lax.broadcasted_iota(jnp.int32, yp.shape, 1)
            first = lane < HEAD
            zero = jnp.zeros_like(yp)
            self._hi = jnp.concatenate([jnp.where(first, yp, zero), jnp.where(first, zero, yp)], axis=0)
        return self._hi


def _mm(terms):
    la = jnp.concatenate([a.hi for a, _ in terms], axis=1)
    lb = jnp.concatenate([b.hi for _, b in terms], axis=0)
    return jnp.dot(la, lb, preferred_element_type=F32)


def _shift_rows(x, carry, n):
    row = lax.broadcasted_iota(jnp.int32, x.shape, 0)
    out = pltpu.roll(x, n, axis=0)
    for i in range(n):
        src = SUBLANES - n + i
        out = jnp.where(row == i, carry[src:src + 1, :], out)
    return out


def _sigmoid(x):
    return 0.5 + 0.5 * jnp.tanh(0.5 * x)


def _silu(x):
    h = 0.5 * x
    return h + h * jnp.tanh(h)


def _interleave(order, stages):
    for name in order:
        if name in stages:
            next(stages[name], None)
    for gen in stages.values():
        for _ in gen:
            pass


_HALF_ORDER = ("prep", "inverse", "prep", "inverse", "prep", "inverse", "inverse", "prep", "inverse", "advance",
               "inverse", "advance", "inverse", "advance", "finish", "inverse", "advance", "finish", "inverse",
               "finish", "inverse", "inverse", "prep")


def _mixer_body(pm_ref, lm_ref, px_ref, lx_ref, mu_ref, vec_ref, wl_ref, ones_ref, tri_ref, o_ref,
                carry_rw, carry_u, state, s_at, s_rt, s_kh, s_bh, s_v, s_pe, s_y,
                r_x, r_tak, r_ark, r_arb, e_bonus, e_gate, e_yb):
    def vec(i):
        return vec_ref[i:i + 1, :]

    w0, a0, k_k, k_a, r_k, lnx_g, lnx_b, cw0, cw1, cw2 = (vec(i) for i in range(10))
    ps = range(N_PAIRS)

    def seg_sums(xs):
        tiles = [x[:, p * LANES:(p + 1) * LANES] for x in xs for p in ps]
        t = jnp.concatenate(tiles, axis=0).astype(BF16)
        s = jnp.dot(t, ones_ref[...], preferred_element_type=F32)
        out = []
        for i in range(len(xs)):
            rows = [s[(i * N_PAIRS + p) * CHUNK:(i * N_PAIRS + p + 1) * CHUNK] for p in ps]
            out.append(jnp.concatenate(rows, axis=1))
        return out

    def prep(p_ref, l_ref, row0, sel):
        p_rw = jnp.concatenate([p_ref[0, row0:row0 + CHUNK, :4 * D_RWKV], l_ref[0, row0:row0 + CHUNK, :]],
                               axis=1)
        prev = _shift_rows(p_rw, carry_rw[:, :RW_COLS], 1)
        carry_rw[:, :RW_COLS] = p_rw[CHUNK - SUBLANES:, :]
        pm = p_rw + (prev - p_rw) * mu_ref[:, :RW_COLS]
        r = pm[:, 0 * D_RWKV:1 * D_RWKV]
        k = pm[:, 1 * D_RWKV:2 * D_RWKV]
        v = pm[:, 2 * D_RWKV:3 * D_RWKV]
        g_r = pm[:, 3 * D_RWKV:4 * D_RWKV]
        lora_in = pm[:, 4 * D_RWKV:]
        lane = lax.broadcasted_iota(jnp.int32, lora_in.shape, 1)
        lora_in = jnp.where(lane < D_LORA, jnp.tanh(lora_in), lora_in).astype(BF16)
        gate = _silu(g_r)

        p_cv = p_ref[0, row0:row0 + CHUNK, 4 * D_RWKV:]
        b_g = p_cv[:, 0 * D_CONV:1 * D_CONV]
        c_g = p_cv[:, 1 * D_CONV:2 * D_CONV]
        h_c = p_cv[:, 2 * D_CONV:3 * D_CONV]
        g_c = p_cv[:, 3 * D_CONV:4 * D_CONV]
        u_c = c_g * h_c
        carry = carry_u[...]
        conv = cw0 * _shift_rows(u_c, carry, 2) + cw1 * _shift_rows(u_c, carry, 1) + cw2 * u_c
        carry_u[...] = u_c[CHUNK - SUBLANES:, :]
        y_b = (b_g * conv * _silu(g_c)).astype(e_yb.dtype)
        yield
        lora = jnp.dot(lora_in, wl_ref[...], preferred_element_type=F32)
        logw = (-math.exp(-0.5)) * _sigmoid(w0 + lora[:, :D_RWKV])
        a = _sigmoid(a0 + lora[:, D_RWKV:])
        kk = k * k_k
        k = k * (1.0 + (a - 1.0) * k_a)
        yield
        logw = _Split(logw)
        cum = jnp.dot(tri_ref[...], jnp.concatenate([logw.hi, logw.lo], axis=0),
                      preferred_element_type=F32)
        logw = logw.x
        e_in = jnp.exp(cum)
        e_out = jnp.exp(-cum)
        e_ex = jnp.exp(cum - logw)
        p_end = e_in[CHUNK - 1:CHUNK, :]
        r_t = r * e_in
        k_h = k * e_out
        yield
        kk_sq, rk_sum = seg_sums([kk * kk, r * k * r_k])
        kk = kk * lax.rsqrt(jnp.maximum(kk_sq, 1e-24))
        a_t = kk * e_ex
        b_h = kk * a * e_out
        bonus = rk_sum * v
        outs = [val.astype(dst.dtype) for dst, val in ((s_at, a_t), (s_rt, r_t), (s_kh, k_h), (s_bh, b_h), (s_v, v))]
        yield
        for dst, val in zip((s_at, s_rt, s_kh, s_bh, s_v), outs):
            for p in ps:
                dst[sel, p] = val[:, p * LANES:(p + 1) * LANES]
        for p in ps:
            s_pe[sel, p] = jnp.broadcast_to(p_end[:, p * LANES:(p + 1) * LANES], (SUBLANES, LANES))
        e_bonus[sel] = bonus
        e_gate[sel] = gate
        e_yb[sel] = y_b

    t_i = lax.broadcasted_iota(jnp.int32, (CHUNK, LANES), 0)
    j_i = lax.broadcasted_iota(jnp.int32, (CHUNK, LANES), 1) & (HEAD - 1)

    def splits(xs):
        return [_Split(x) for x in xs]

    def pmm(xs, ys):
        return [_mm([(x, _BlockDiag(y))]) for x, y in zip(xs, ys)]

    def inverse(sel):
        strict = j_i < t_i
        incl = j_i <= t_i
        eye = (j_i == t_i).astype(F32)

        def same_block(log2_size):
            return (j_i >> log2_size) == (t_i >> log2_size)

        g = []
        for p in ps:
            lhs = jnp.concatenate([s_at[sel, p], s_rt[sel, p]], axis=0)
            rhs_t = jnp.concatenate([_BlockDiag(_Split(s_bh[sel, p])).hi,
                                     _BlockDiag(_Split(s_kh[sel, p])).hi], axis=0)
            g.append(lax.dot_general(lhs, rhs_t, (((1,), (1,)), ((), ())), preferred_element_type=F32))
        t_ab = [jnp.where(strict, g_[:CHUNK, :LANES], 0.0) for g_ in g]
        for p in ps:
            r_arb[sel, p] = jnp.where(incl, g[p][CHUNK:, :LANES], 0.0).astype(r_arb.dtype)
            r_tak[sel, p] = jnp.where(strict, g[p][:CHUNK, LANES:], 0.0).astype(r_tak.dtype)
            r_ark[sel, p] = jnp.where(incl, g[p][CHUNK:, LANES:], 0.0).astype(r_ark.dtype)
        yield

        x = splits([eye - jnp.where(same_block(1), t, 0.0) for t in t_ab])
        for log2_size in range(2, 7):
            level = same_block(log2_size) & jnp.logical_not(same_block(log2_size - 1))
            c = splits([jnp.where(level, t, 0.0) for t in t_ab])
            xc = splits(pmm(x, c))
            yield
            x = splits([x_.x - m for x_, m in zip(x, pmm(xc, x))])
            if log2_size < 6:
                yield
        for p in ps:
            r_x[sel, p] = x[p].hi

    def advance(sel):
        row = lax.broadcasted_iota(jnp.int32, (LANES, LANES), 0)
        col = lax.broadcasted_iota(jnp.int32, (LANES, LANES), 1)
        same_head = (row < HEAD) == (col < HEAD)
        vv = [s_v[sel, p] for p in ps]
        vbd = [_BlockDiag(_Split(v_)) for v_ in vv]
        s0 = [state[p] for p in ps]
        h0 = splits([s.T for s in s0])
        rhs = splits([_mm([(_Split(s_at[sel, p]), h0[p]), (_Split(r_tak[sel, p]), vbd[p])]) for p in ps])
        yield
        u = splits(pmm(splits([r_x[sel, p] for p in ps]), rhs))
        yield
        for p in ps:
            s_y[sel, p] = _mm([(_Split(s_rt[sel, p]), h0[p]), (_Split(r_ark[sel, p]), vbd[p]),
                               (_Split(-r_arb[sel, p]), _BlockDiag(u[p]))])
        yield
        for p in ps:
            wv_t = _Split(jnp.concatenate([vv[p], u[p].x], axis=0).T)
            upd = _mm([(wv_t, _Split(jnp.concatenate([s_kh[sel, p], -s_bh[sel, p]], axis=0)))])
            state[p] = jnp.where(same_head, (s0[p] + upd) * s_pe[sel, p][0:1, :], 0.0)

    def finish(sel, row0):
        y = jnp.concatenate([s_y[sel, p] for p in ps], axis=1)
        inv_n = 1.0 / HEAD
        yield
        mean = seg_sums([y])[0] * inv_n
        yc = y - mean
        yield
        var = seg_sums([yc * yc])[0] * inv_n
        yn = yc * lax.rsqrt(var + LNX_EPS) * lnx_g + lnx_b
        y_a = (yn + e_bonus[sel]) * e_gate[sel]
        o_ref[0, row0:row0 + CHUNK, :D_RWKV] = y_a.astype(o_ref.dtype)
        o_ref[0, row0:row0 + CHUNK, D_RWKV:] = e_yb[sel]

    @pl.when(pl.program_id(1) == 0)
    def _():
        for ref in (carry_rw, carry_u, state, s_at, s_rt, s_kh, s_bh, s_v, s_pe,
                    r_x, r_tak, r_ark, r_arb, e_bonus, e_gate, e_yb):
            ref[...] = jnp.zeros_like(ref)
        for _ in prep(pm_ref, lm_ref, 0, 0):
            pass

    is_drain = pl.program_id(1) == pl.num_programs(1) - 1

    @pl.when(jnp.logical_not(is_drain))
    def _():
        _interleave(_HALF_ORDER, {"advance": advance(1), "finish": finish(1, 0), "inverse": inverse(0),
                                  "prep": prep(px_ref, lx_ref, 0, 1)})
        _interleave(_HALF_ORDER, {"advance": advance(0), "finish": finish(0, CHUNK), "inverse": inverse(1),
                                  "prep": prep(px_ref, lx_ref, CHUNK, 0)})

    @pl.when(is_drain)
    def _():
        _interleave(_HALF_ORDER, {"advance": advance(1), "finish": finish(1, 0), "inverse": inverse(0)})
        _interleave(_HALF_ORDER, {"advance": advance(0), "finish": finish(0, CHUNK)})


def _mixer(proj_meta, lora_meta, proj, lora, mu, vecs, w_lora, seg_ones, tri2):
    bsz, seq, _ = proj.shape
    assert seq % (2 * CHUNK) == 0
    n_steps = seq // (2 * CHUNK) + 1
    pair_buf = pltpu.VMEM((2, N_PAIRS, CHUNK, LANES), F32)
    pair_bf16 = pltpu.VMEM((2, N_PAIRS, CHUNK, LANES), BF16)
    row_buf = pltpu.VMEM((2, CHUNK, D_RWKV), F32)
    const = lambda shape: pl.BlockSpec(shape, lambda b, j: (0,) * len(shape))
    return pl.pallas_call(
        _mixer_body,
        grid=(bsz, n_steps),
        in_specs=[
            const(proj_meta.shape), const(lora_meta.shape),
            pl.BlockSpec((1, 2 * CHUNK, MAIN_COLS), lambda b, j: (b, jnp.minimum(j, n_steps - 2), 0)),
            pl.BlockSpec((1, 2 * CHUNK, LORA_COLS), lambda b, j: (b, jnp.minimum(j, n_steps - 2), 0)),
            const(mu.shape), const(vecs.shape), const(w_lora.shape),
            const(seg_ones.shape), const(tri2.shape),
        ],
        out_specs=pl.BlockSpec((1, 2 * CHUNK, D_MODEL), lambda b, j: (b, jnp.maximum(j - 1, 0), 0)),
        out_shape=jax.ShapeDtypeStruct((bsz, seq, D_MODEL), BF16),
        scratch_shapes=[
            pltpu.VMEM((SUBLANES, RW_PAD), F32),
            pltpu.VMEM((SUBLANES, D_CONV), F32),
            pltpu.VMEM((N_PAIRS, LANES, LANES), F32),
            pair_bf16, pair_bf16, pair_bf16, pair_bf16, pair_buf,
            pltpu.VMEM((2, N_PAIRS, SUBLANES, LANES), F32),
            pair_buf,
            pair_bf16, pair_bf16, pair_bf16, pair_bf16,
            row_buf, row_buf,
            pltpu.VMEM((2, CHUNK, D_CONV), BF16),
        ],
        compiler_params=pltpu.CompilerParams(
            dimension_semantics=("arbitrary", "arbitrary"),
            vmem_limit_bytes=VMEM_LIMIT),
        name="mixer",
    )(proj_meta, lora_meta, proj, lora, mu, vecs, w_lora, seg_ones, tri2)


def _outproj_body(mix_ref, x_ref, w_ref, g_ref, o_ref):
    h = x_ref[...] + jnp.dot(mix_ref[...], w_ref[...], preferred_element_type=F32)
    ms = jnp.mean(h * h, axis=-1, keepdims=True)
    o_ref[...] = h * lax.rsqrt(ms + RMS_EPS) * g_ref[...]


def _outproj(mix2d, x2d, w_bf16, g):
    rows = x2d.shape[0]
    tm = OUTPROJ_TM
    assert rows % tm == 0
    return pl.pallas_call(
        _outproj_body,
        grid=(rows // tm,),
        in_specs=[
            pl.BlockSpec((tm, D_MODEL), lambda i: (i, 0)),
            pl.BlockSpec((tm, D_MODEL), lambda i: (i, 0)),
            pl.BlockSpec((D_MODEL, D_MODEL), lambda i: (0, 0), pipeline_mode=pl.Buffered(1)),
            pl.BlockSpec((1, D_MODEL), lambda i: (0, 0)),
        ],
        out_specs=pl.BlockSpec((tm, D_MODEL), lambda i: (i, 0)),
        out_shape=jax.ShapeDtypeStruct((rows, D_MODEL), F32),
        compiler_params=pltpu.CompilerParams(
            dimension_semantics=("arbitrary",),
            vmem_limit_bytes=PROJ_VMEM_LIMIT),
        name="outproj",
    )(mix2d, x2d, w_bf16, g)


def kernel(x, meta_tokens, norm_in_g, w_in, mu_shift, w0, w_lora_up, a0, a_lora_up, k_k, k_a, r_k, lnx_g, lnx_b, conv_w, w_out, norm_f_g):
    bsz, seq, _ = x.shape
    assert norm_in_g.shape[0] == 1 and seq % CHUNK == 0
    x2d = x.reshape(bsz * seq, D_MODEL)

    front = jnp.concatenate([jnp.zeros((FRONT_PAD, D_MODEL), x.dtype), meta_tokens.astype(x.dtype)], axis=0)
    g_in = norm_in_g[0][None, :]
    meta_all, w_in_bf16 = _inproj_meta(front, g_in, w_in[0])
    proj_meta = jnp.concatenate([meta_all[..., :4 * D_RWKV], meta_all[..., RW_COLS:]], axis=-1)
    lora_meta = meta_all[..., 4 * D_RWKV:RW_COLS]
    proj, lora = _inproj(x, g_in, w_in_bf16)

    w_lora = jnp.zeros((2 * D_LORA, 2 * D_RWKV), F32)
    w_lora = w_lora.at[:D_LORA, :D_RWKV].set(w_lora_up[0]).at[D_LORA:, D_RWKV:].set(a_lora_up[0])
    vecs = jnp.concatenate([
        w0[0][None], a0[0][None], k_k[0][None], k_a[0][None], r_k[0].reshape(1, D_RWKV),
        lnx_g[0][None], lnx_b[0][None], conv_w[0],
        jnp.zeros((16 - 10, D_RWKV), F32)], axis=0)
    lane_head = jnp.arange(LANES) // HEAD
    seg_ones = (lane_head[:, None] == lane_head[None, :]).astype(BF16)
    t_idx = jnp.arange(CHUNK)
    tri = (t_idx[None, :] <= t_idx[:, None]).astype(BF16)
    tri2 = jnp.concatenate([tri, tri], axis=1)

    mu_pad = jnp.pad(mu_shift, ((0, 0), (0, RW_PAD - RW_COLS)))
    mix = _mixer(proj_meta, lora_meta, proj, lora, mu_pad, vecs, w_lora.astype(BF16), seg_ones, tri2)

    out = _outproj(mix.reshape(bsz * seq, D_MODEL), x2d, w_out[0].astype(BF16), norm_f_g[None, :])
    return out.reshape(bsz, seq, D_MODEL)
```

```python
import math

import jax
import jax.numpy as jnp
from jax import lax
from jax.experimental import pallas as pl
from jax.experimental.pallas import tpu as pltpu

F32 = jnp.float32
BF16 = jnp.bfloat16

D_MODEL = 2048
N_META = 16
D_RWKV = 1024
HEAD = 64
N_HEADS = D_RWKV // HEAD
D_LORA = 64
D_CONV = 1024
RW_COLS = 4 * D_RWKV + 2 * D_LORA
IN_COLS = RW_COLS + 4 * D_CONV
LORA_COLS = 2 * D_LORA
MAIN_COLS = IN_COLS - LORA_COLS
RMS_EPS = 1e-6
LNX_EPS = 64e-5

CHUNK = 64
LANES = 128
SUBLANES = 8
N_PAIRS = D_RWKV // LANES
FRONT_PAD = CHUNK - N_META

VMEM_LIMIT = 56 * 1024 * 1024
PROJ_VMEM_LIMIT = 60 * 1024 * 1024

INPROJ_TM = 1024
INPROJ_TN = 2048
META_TN = 1664
OUTPROJ_TM = 1024


def _rmsnorm_bf16(x, g):
    ms = jnp.mean(x * x, axis=-1, keepdims=True)
    return (x * lax.rsqrt(ms + RMS_EPS) * g).astype(BF16)


def _inproj_body(x_ref, g_ref, w_ref, wl_ref, o_ref, ol_ref, hn_ref):
    @pl.when(pl.program_id(2) == 0)
    def _():
        hn_ref[...] = _rmsnorm_bf16(x_ref[0], g_ref[...])
        ol_ref[0] = jnp.dot(hn_ref[...], wl_ref[...], preferred_element_type=F32)

    o_ref[0] = jnp.dot(hn_ref[...], w_ref[...], preferred_element_type=F32)


def _main_col_start(j):
    skip = jnp.where(j * INPROJ_TN >= 4 * D_RWKV, LORA_COLS // LANES, 0)
    return (j * (INPROJ_TN // LANES) + skip) * LANES


def _inproj(x, g, w_bf16):
    bsz, seq, _ = x.shape
    tm, tn = INPROJ_TM, INPROJ_TN
    assert seq % tm == 0 and MAIN_COLS % tn == 0 and (4 * D_RWKV) % tn == 0
    return pl.pallas_call(
        _inproj_body,
        grid=(bsz, seq // tm, MAIN_COLS // tn),
        in_specs=[
            pl.BlockSpec((1, tm, D_MODEL), lambda b, i, j: (b, i, 0)),
            pl.BlockSpec((1, D_MODEL), lambda b, i, j: (0, 0)),
            pl.BlockSpec((pl.Element(D_MODEL), pl.Element(tn)), lambda b, i, j: (0, _main_col_start(j))),
            pl.BlockSpec((D_MODEL, LORA_COLS), lambda b, i, j: (0, 4 * D_RWKV // LORA_COLS)),
        ],
        out_specs=[pl.BlockSpec((1, tm, tn), lambda b, i, j: (b, i, j)),
                   pl.BlockSpec((1, tm, LORA_COLS), lambda b, i, j: (b, i, 0))],
        out_shape=[jax.ShapeDtypeStruct((bsz, seq, MAIN_COLS), F32),
                   jax.ShapeDtypeStruct((bsz, seq, LORA_COLS), F32)],
        scratch_shapes=[pltpu.VMEM((tm, D_MODEL), BF16)],
        compiler_params=pltpu.CompilerParams(
            dimension_semantics=("arbitrary", "arbitrary", "arbitrary"),
            vmem_limit_bytes=PROJ_VMEM_LIMIT),
        name="inproj",
    )(x, g, w_bf16, w_bf16)


def _inproj_meta_body(x_ref, g_ref, w_ref, o_ref, wb_ref):
    w = w_ref[...].astype(BF16)
    wb_ref[...] = w
    o_ref[0] = jnp.dot(_rmsnorm_bf16(x_ref[...], g_ref[...]), w, preferred_element_type=F32)


def _inproj_meta(front, g, w_f32):
    tn = META_TN
    assert IN_COLS % tn == 0
    return pl.pallas_call(
        _inproj_meta_body,
        grid=(IN_COLS // tn,),
        in_specs=[
            pl.BlockSpec((CHUNK, D_MODEL), lambda j: (0, 0)),
            pl.BlockSpec((1, D_MODEL), lambda j: (0, 0)),
            pl.BlockSpec((D_MODEL, tn), lambda j: (0, j)),
        ],
        out_specs=[pl.BlockSpec((1, CHUNK, tn), lambda j: (0, 0, j)),
                   pl.BlockSpec((D_MODEL, tn), lambda j: (0, j))],
        out_shape=[jax.ShapeDtypeStruct((1, CHUNK, IN_COLS), F32),
                   jax.ShapeDtypeStruct((D_MODEL, IN_COLS), BF16)],
        compiler_params=pltpu.CompilerParams(
            dimension_semantics=("arbitrary",),
            vmem_limit_bytes=VMEM_LIMIT),
        name="inproj_meta",
    )(front, g, w_f32)


class _Split:
    def __init__(self, x):
        self.x = x
        self._hi = None
        self._lo = None

    @property
    def hi(self):
        if self._hi is None:
            self._hi = self.x.astype(BF16)
        return self._hi

    @property
    def lo(self):
        if self._lo is None:
            self._lo = (self.x - self.hi.astype(F32)).astype(BF16)
        return self._lo


class _BlockDiag:
    def __init__(self, y):
        self._y = y
        self._hi = None

    @property
    def hi(self):
        if self._hi is None:
            yp = self._y.hi
            lane = lax.broadcasted_iota(jnp.int32, yp.shape, 1)
            first = lane < HEAD
            zero = jnp.zeros_like(yp)
            self._hi = jnp.concatenate([jnp.where(first, yp, zero), jnp.where(first, zero, yp)], axis=0)
        return self._hi


def _mm(terms):
    la = jnp.concatenate([a.hi for a, _ in terms], axis=1)
    lb = jnp.concatenate([b.hi for _, b in terms], axis=0)
    return jnp.dot(la, lb, preferred_element_type=F32)


def _shift_rows(x, carry, n):
    row = lax.broadcasted_iota(jnp.int32, x.shape, 0)
    out = pltpu.roll(x, n, axis=0)
    for i in range(n):
        src = SUBLANES - n + i
        out = jnp.where(row == i, carry[src:src + 1, :], out)
    return out


def _sigmoid(x):
    return 0.5 + 0.5 * jnp.tanh(0.5 * x)


def _silu(x):
    h = 0.5 * x
    return h + h * jnp.tanh(h)


def _interleave(order, stages):
    for name in order:
        if name in stages:
            next(stages[name], None)
    for gen in stages.values():
        for _ in gen:
            pass


_HALF_ORDER = ("prep", "inverse", "prep", "inverse", "prep", "inverse", "inverse", "prep", "inverse", "advance",
               "inverse", "advance", "inverse", "advance", "finish", "inverse", "advance", "finish", "inverse",
               "finish", "inverse", "inverse", "prep")


def _mixer_body(pm_ref, lm_ref, px_ref, lx_ref, mu_ref, vec_ref, wl_ref, ones_ref, tri_ref, o_ref,
                carry_rw, carry_u, state, s_at, s_rt, s_kh, s_bh, s_v, s_pe, s_y,
                r_x, r_tak, r_ark, r_arb, e_bonus, e_gate, e_yb):
    def vec(i):
        return vec_ref[i:i + 1, :]

    w0, a0, k_k, k_a, r_k, lnx_g, lnx_b, cw0, cw1, cw2 = (vec(i) for i in range(10))
    ps = range(N_PAIRS)

    def seg_sums(xs):
        tiles = [x[:, p * LANES:(p + 1) * LANES] for x in xs for p in ps]
        t = jnp.concatenate(tiles, axis=0).astype(BF16)
        s = jnp.dot(t, ones_ref[...], preferred_element_type=F32)
        out = []
        for i in range(len(xs)):
            rows = [s[(i * N_PAIRS + p) * CHUNK:(i * N_PAIRS + p + 1) * CHUNK] for p in ps]
            out.append(jnp.concatenate(rows, axis=1))
        return out

    def prep(p_ref, l_ref, row0, sel):
        p_rw = jnp.concatenate([p_ref[0, row0:row0 + CHUNK, :4 * D_RWKV], l_ref[0, row0:row0 + CHUNK, :]],
                               axis=1)
        prev = _shift_rows(p_rw, carry_rw[...], 1)
        carry_rw[...] = p_rw[CHUNK - SUBLANES:, :]
        pm = p_rw + (prev - p_rw) * mu_ref[...]
        r = pm[:, 0 * D_RWKV:1 * D_RWKV]
        k = pm[:, 1 * D_RWKV:2 * D_RWKV]
        v = pm[:, 2 * D_RWKV:3 * D_RWKV]
        g_r = pm[:, 3 * D_RWKV:4 * D_RWKV]
        lora_in = pm[:, 4 * D_RWKV:]
        lane = lax.broadcasted_iota(jnp.int32, lora_in.shape, 1)
        lora_in = jnp.where(lane < D_LORA, jnp.tanh(lora_in), lora_in).astype(BF16)
        gate = _silu(g_r)

        p_cv = p_ref[0, row0:row0 + CHUNK, 4 * D_RWKV:]
        b_g = p_cv[:, 0 * D_CONV:1 * D_CONV]
        c_g = p_cv[:, 1 * D_CONV:2 * D_CONV]
        h_c = p_cv[:, 2 * D_CONV:3 * D_CONV]
        g_c = p_cv[:, 3 * D_CONV:4 * D_CONV]
        u_c = c_g * h_c
        carry = carry_u[...]
        conv = cw0 * _shift_rows(u_c, carry, 2) + cw1 * _shift_rows(u_c, carry, 1) + cw2 * u_c
        carry_u[...] = u_c[CHUNK - SUBLANES:, :]
        y_b = (b_g * conv * _silu(g_c)).astype(e_yb.dtype)
        yield
        lora = jnp.dot(lora_in, wl_ref[...], preferred_element_type=F32)
        logw = (-math.exp(-0.5)) * _sigmoid(w0 + lora[:, :D_RWKV])
        a = _sigmoid(a0 + lora[:, D_RWKV:])
        kk = k * k_k
        k = k * (1.0 + (a - 1.0) * k_a)
        yield
        logw = _Split(logw)
        cum = jnp.dot(tri_ref[...], jnp.concatenate([logw.hi, logw.lo], axis=0),
                      preferred_element_type=F32)
        logw = logw.x
        e_in = jnp.exp(cum)
        e_out = jnp.exp(-cum)
        e_ex = jnp.exp(cum - logw)
        p_end = e_in[CHUNK - 1:CHUNK, :]
        r_t = r * e_in
        k_h = k * e_out
        yield
        kk_sq, rk_sum = seg_sums([kk * kk, r * k * r_k])
        kk = kk * lax.rsqrt(jnp.maximum(kk_sq, 1e-24))
        a_t = kk * e_ex
        b_h = kk * a * e_out
        bonus = rk_sum * v
        outs = [val.astype(dst.dtype) for dst, val in ((s_at, a_t), (s_rt, r_t), (s_kh, k_h), (s_bh, b_h), (s_v, v))]
        yield
        for dst, val in zip((s_at, s_rt, s_kh, s_bh, s_v), outs):
            for p in ps:
                dst[sel, p] = val[:, p * LANES:(p + 1) * LANES]
        for p in ps:
            s_pe[sel, p] = jnp.broadcast_to(p_end[:, p * LANES:(p + 1) * LANES], (SUBLANES, LANES))
        e_bonus[sel] = bonus
        e_gate[sel] = gate
        e_yb[sel] = y_b

    t_i = lax.broadcasted_iota(jnp.int32, (CHUNK, LANES), 0)
    j_i = lax.broadcasted_iota(jnp.int32, (CHUNK, LANES), 1) & (HEAD - 1)

    def splits(xs):
        return [_Split(x) for x in xs]

    def pmm(xs, ys):
        return [_mm([(x, _BlockDiag(y))]) for x, y in zip(xs, ys)]

    def inverse(sel):
        strict = j_i < t_i
        incl = j_i <= t_i
        eye = (j_i == t_i).astype(F32)

        def same_block(log2_size):
            return (j_i >> log2_size) == (t_i >> log2_size)

        g = []
        for p in ps:
            lhs = jnp.concatenate([s_at[sel, p], s_rt[sel, p]], axis=0)
            rhs_t = jnp.concatenate([_BlockDiag(_Split(s_bh[sel, p])).hi,
                                     _BlockDiag(_Split(s_kh[sel, p])).hi], axis=0)
            g.append(lax.dot_general(lhs, rhs_t, (((1,), (1,)), ((), ())), preferred_element_type=F32))
        t_ab = [jnp.where(strict, g_[:CHUNK, :LANES], 0.0) for g_ in g]
        for p in ps:
            r_arb[sel, p] = jnp.where(incl, g[p][CHUNK:, :LANES], 0.0).astype(r_arb.dtype)
            r_tak[sel, p] = jnp.where(strict, g[p][:CHUNK, LANES:], 0.0).astype(r_tak.dtype)
            r_ark[sel, p] = jnp.where(incl, g[p][CHUNK:, LANES:], 0.0).astype(r_ark.dtype)
        yield

        x = splits([eye - jnp.where(same_block(1), t, 0.0) for t in t_ab])
        for log2_size in range(2, 7):
            level = same_block(log2_size) & jnp.logical_not(same_block(log2_size - 1))
            c = splits([jnp.where(level, t, 0.0) for t in t_ab])
            xc = splits(pmm(x, c))
            yield
            x = splits([x_.x - m for x_, m in zip(x, pmm(xc, x))])
            if log2_size < 6:
                yield
        for p in ps:
            r_x[sel, p] = x[p].hi

    def advance(sel):
        row = lax.broadcasted_iota(jnp.int32, (LANES, LANES), 0)
        col = lax.broadcasted_iota(jnp.int32, (LANES, LANES), 1)
        same_head = (row < HEAD) == (col < HEAD)
        vv = [s_v[sel, p] for p in ps]
        vbd = [_BlockDiag(_Split(v_)) for v_ in vv]
        s0 = [state[p] for p in ps]
        h0 = splits([s.T for s in s0])
        rhs = splits([_mm([(_Split(s_at[sel, p]), h0[p]), (_Split(r_tak[sel, p]), vbd[p])]) for p in ps])
        yield
        u = splits(pmm(splits([r_x[sel, p] for p in ps]), rhs))
        yield
        for p in ps:
            s_y[sel, p] = _mm([(_Split(s_rt[sel, p]), h0[p]), (_Split(r_ark[sel, p]), vbd[p]),
                               (_Split(-r_arb[sel, p]), _BlockDiag(u[p]))])
        yield
        for p in ps:
            wv_t = _Split(jnp.concatenate([vv[p], u[p].x], axis=0).T)
            upd = _mm([(wv_t, _Split(jnp.concatenate([s_kh[sel, p], -s_bh[sel, p]], axis=0)))])
            state[p] = jnp.where(same_head, (s0[p] + upd) * s_pe[sel, p][0:1, :], 0.0)

    def finish(sel, row0):
        y = jnp.concatenate([s_y[sel, p] for p in ps], axis=1)
        inv_n = 1.0 / HEAD
        yield
        mean = seg_sums([y])[0] * inv_n
        yc = y - mean
        yield
        var = seg_sums([yc * yc])[0] * inv_n
        yn = yc * lax.rsqrt(var + LNX_EPS) * lnx_g + lnx_b
        y_a = (yn + e_bonus[sel]) * e_gate[sel]
        o_ref[0, row0:row0 + CHUNK, :D_RWKV] = y_a.astype(o_ref.dtype)
        o_ref[0, row0:row0 + CHUNK, D_RWKV:] = e_yb[sel]

    is_first = pl.program_id(1) == 0
    is_drain = pl.program_id(1) == pl.num_programs(1) - 1

    @pl.when(is_first)
    def _():
        for ref in (carry_rw, carry_u, state):
            ref[...] = jnp.zeros_like(ref)
        for _ in prep(pm_ref, lm_ref, 0, 0):
            pass
        _interleave(_HALF_ORDER, {"inverse": inverse(0), "prep": prep(px_ref, lx_ref, 0, 1)})
        _interleave(_HALF_ORDER, {"advance": advance(0), "inverse": inverse(1), "prep": prep(px_ref, lx_ref, CHUNK, 0)})

    @pl.when(jnp.logical_not(is_first | is_drain))
    def _():
        _interleave(_HALF_ORDER, {"advance": advance(1), "finish": finish(1, 0), "inverse": inverse(0),
                                  "prep": prep(px_ref, lx_ref, 0, 1)})
        _interleave(_HALF_ORDER, {"advance": advance(0), "finish": finish(0, CHUNK), "inverse": inverse(1),
                                  "prep": prep(px_ref, lx_ref, CHUNK, 0)})

    @pl.when(is_drain)
    def _():
        _interleave(_HALF_ORDER, {"advance": advance(1), "finish": finish(1, 0), "inverse": inverse(0)})
        _interleave(_HALF_ORDER, {"advance": advance(0), "finish": finish(0, CHUNK)})


def _mixer(proj_meta, lora_meta, proj, lora, mu, vecs, w_lora, seg_ones, tri2):
    bsz, seq, _ = proj.shape
    assert seq % (2 * CHUNK) == 0
    n_steps = seq // (2 * CHUNK) + 1
    pair_buf = pltpu.VMEM((2, N_PAIRS, CHUNK, LANES), F32)
    pair_bf16 = pltpu.VMEM((2, N_PAIRS, CHUNK, LANES), BF16)
    row_buf = pltpu.VMEM((2, CHUNK, D_RWKV), F32)
    const = lambda shape: pl.BlockSpec(shape, lambda b, j: (0,) * len(shape))
    return pl.pallas_call(
        _mixer_body,
        grid=(bsz, n_steps),
        in_specs=[
            const(proj_meta.shape), const(lora_meta.shape),
            pl.BlockSpec((1, 2 * CHUNK, MAIN_COLS), lambda b, j: (b, jnp.minimum(j, n_steps - 2), 0)),
            pl.BlockSpec((1, 2 * CHUNK, LORA_COLS), lambda b, j: (b, jnp.minimum(j, n_steps - 2), 0)),
            const(mu.shape), const(vecs.shape), const(w_lora.shape),
            const(seg_ones.shape), const(tri2.shape),
        ],
        out_specs=pl.BlockSpec((1, 2 * CHUNK, D_MODEL), lambda b, j: (b, jnp.maximum(j - 1, 0), 0)),
        out_shape=jax.ShapeDtypeStruct((bsz, seq, D_MODEL), BF16),
        scratch_shapes=[
            pltpu.VMEM((SUBLANES, RW_COLS), F32),
            pltpu.VMEM((SUBLANES, D_CONV), F32),
            pltpu.VMEM((N_PAIRS, LANES, LANES), F32),
            pair_bf16, pair_bf16, pair_bf16, pair_bf16, pair_buf,
            pltpu.VMEM((2, N_PAIRS, SUBLANES, LANES), F32),
            pair_buf,
            pair_bf16, pair_bf16, pair_bf16, pair_bf16,
            row_buf, row_buf,
            pltpu.VMEM((2, CHUNK, D_CONV), BF16),
        ],
        compiler_params=pltpu.CompilerParams(
            dimension_semantics=("arbitrary", "arbitrary"),
            vmem_limit_bytes=VMEM_LIMIT),
        name="mixer",
    )(proj_meta, lora_meta, proj, lora, mu, vecs, w_lora, seg_ones, tri2)


def _outproj_body(mix_ref, x_ref, w_ref, g_ref, o_ref):
    h = x_ref[...] + jnp.dot(mix_ref[...], w_ref[...], preferred_element_type=F32)
    ms = jnp.mean(h * h, axis=-1, keepdims=True)
    o_ref[...] = h * lax.rsqrt(ms + RMS_EPS) * g_ref[...]


def _outproj(mix2d, x2d, w_bf16, g):
    rows = x2d.shape[0]
    tm = OUTPROJ_TM
    assert rows % tm == 0
    return pl.pallas_call(
        _outproj_body,
        grid=(rows // tm,),
        in_specs=[
            pl.BlockSpec((tm, D_MODEL), lambda i: (i, 0)),
            pl.BlockSpec((tm, D_MODEL), lambda i: (i, 0)),
            pl.BlockSpec((D_MODEL, D_MODEL), lambda i: (0, 0), pipeline_mode=pl.Buffered(1)),
            pl.BlockSpec((1, D_MODEL), lambda i: (0, 0)),
        ],
        out_specs=pl.BlockSpec((tm, D_MODEL), lambda i: (i, 0)),
        out_shape=jax.ShapeDtypeStruct((rows, D_MODEL), F32),
        compiler_params=pltpu.CompilerParams(
            dimension_semantics=("arbitrary",),
            vmem_limit_bytes=PROJ_VMEM_LIMIT),
        name="outproj",
    )(mix2d, x2d, w_bf16, g)


def kernel(x, meta_tokens, norm_in_g, w_in, mu_shift, w0, w_lora_up, a0, a_lora_up, k_k, k_a, r_k, lnx_g, lnx_b, conv_w, w_out, norm_f_g):
    bsz, seq, _ = x.shape
    assert norm_in_g.shape[0] == 1 and seq % CHUNK == 0
    x2d = x.reshape(bsz * seq, D_MODEL)

    front = jnp.concatenate([jnp.zeros((FRONT_PAD, D_MODEL), x.dtype), meta_tokens.astype(x.dtype)], axis=0)
    g_in = norm_in_g[0][None, :]
    meta_all, w_in_bf16 = _inproj_meta(front, g_in, w_in[0])
    proj_meta = jnp.concatenate([meta_all[..., :4 * D_RWKV], meta_all[..., RW_COLS:]], axis=-1)
    lora_meta = meta_all[..., 4 * D_RWKV:RW_COLS]
    proj, lora = _inproj(x, g_in, w_in_bf16)

    w_lora = jnp.zeros((2 * D_LORA, 2 * D_RWKV), F32)
    w_lora = w_lora.at[:D_LORA, :D_RWKV].set(w_lora_up[0]).at[D_LORA:, D_RWKV:].set(a_lora_up[0])
    vecs = jnp.concatenate([
        w0[0][None], a0[0][None], k_k[0][None], k_a[0][None], r_k[0].reshape(1, D_RWKV),
        lnx_g[0][None], lnx_b[0][None], conv_w[0],
        jnp.zeros((16 - 10, D_RWKV), F32)], axis=0)
    lane_head = jnp.arange(LANES) // HEAD
    seg_ones = (lane_head[:, None] == lane_head[None, :]).astype(BF16)
    t_idx = jnp.arange(CHUNK)
    tri = (t_idx[None, :] <= t_idx[:, None]).astype(BF16)
    tri2 = jnp.concatenate([tri, tri], axis=1)

    mix = _mixer(proj_meta, lora_meta, proj, lora, mu_shift, vecs, w_lora.astype(BF16), seg_ones, tri2)

    out = _outproj(mix.reshape(bsz * seq, D_MODEL), x2d, w_out[0].astype(BF16), norm_f_g[None, :])
    return out.reshape(bsz, seq, D_MODEL)
```

```python
import math

import jax
import jax.numpy as jnp
from jax import lax
from jax.experimental import pallas as pl
from jax.experimental.pallas import tpu as pltpu

F32 = jnp.float32
BF16 = jnp.bfloat16

D_MODEL = 2048
N_META = 16
D_RWKV = 1024
HEAD = 64
N_HEADS = D_RWKV // HEAD
D_LORA = 64
D_CONV = 1024
RW_COLS = 4 * D_RWKV + 2 * D_LORA
IN_COLS = RW_COLS + 4 * D_CONV
LORA_COLS = 2 * D_LORA
MAIN_COLS = IN_COLS - LORA_COLS
RMS_EPS = 1e-6
LNX_EPS = 64e-5

CHUNK = 64
LANES = 128
SUBLANES = 8
N_PAIRS = D_RWKV // LANES
FRONT_PAD = CHUNK - N_META

VMEM_LIMIT = 56 * 1024 * 1024
PROJ_VMEM_LIMIT = 60 * 1024 * 1024

INPROJ_TM = 1024
INPROJ_TN = 2048
META_TN = 1664
OUTPROJ_TM = 1024


def _rmsnorm_bf16(x, g):
    ms = jnp.mean(x * x, axis=-1, keepdims=True)
    return (x * lax.rsqrt(ms + RMS_EPS) * g).astype(BF16)


def _inproj_body(x_ref, g_ref, w_ref, wl_ref, o_ref, ol_ref, hn_ref):
    @pl.when(pl.program_id(2) == 0)
    def _():
        hn_ref[...] = _rmsnorm_bf16(x_ref[0], g_ref[...])
        ol_ref[0] = jnp.dot(hn_ref[...], wl_ref[...], preferred_element_type=F32)

    o_ref[0] = jnp.dot(hn_ref[...], w_ref[...], preferred_element_type=F32)


def _main_col_start(j):
    skip = jnp.where(j * INPROJ_TN >= 4 * D_RWKV, LORA_COLS // LANES, 0)
    return (j * (INPROJ_TN // LANES) + skip) * LANES


def _inproj(x, g, w_bf16):
    bsz, seq, _ = x.shape
    tm, tn = INPROJ_TM, INPROJ_TN
    assert seq % tm == 0 and MAIN_COLS % tn == 0 and (4 * D_RWKV) % tn == 0
    return pl.pallas_call(
        _inproj_body,
        grid=(bsz, seq // tm, MAIN_COLS // tn),
        in_specs=[
            pl.BlockSpec((1, tm, D_MODEL), lambda b, i, j: (b, i, 0)),
            pl.BlockSpec((1, D_MODEL), lambda b, i, j: (0, 0)),
            pl.BlockSpec((pl.Element(D_MODEL), pl.Element(tn)), lambda b, i, j: (0, _main_col_start(j))),
            pl.BlockSpec((D_MODEL, LORA_COLS), lambda b, i, j: (0, 4 * D_RWKV // LORA_COLS)),
        ],
        out_specs=[pl.BlockSpec((1, tm, tn), lambda b, i, j: (b, i, j)),
                   pl.BlockSpec((1, tm, LORA_COLS), lambda b, i, j: (b, i, 0))],
        out_shape=[jax.ShapeDtypeStruct((bsz, seq, MAIN_COLS), F32),
                   jax.ShapeDtypeStruct((bsz, seq, LORA_COLS), F32)],
        scratch_shapes=[pltpu.VMEM((tm, D_MODEL), BF16)],
        compiler_params=pltpu.CompilerParams(
            dimension_semantics=("arbitrary", "arbitrary", "arbitrary"),
            vmem_limit_bytes=PROJ_VMEM_LIMIT),
        name="inproj",
    )(x, g, w_bf16, w_bf16)


def _inproj_meta_body(x_ref, g_ref, w_ref, o_ref, wb_ref):
    w = w_ref[...].astype(BF16)
    wb_ref[...] = w
    o_ref[0] = jnp.dot(_rmsnorm_bf16(x_ref[...], g_ref[...]), w, preferred_element_type=F32)


def _inproj_meta(front, g, w_f32):
    tn = META_TN
    assert IN_COLS % tn == 0
    return pl.pallas_call(
        _inproj_meta_body,
        grid=(IN_COLS // tn,),
        in_specs=[
            pl.BlockSpec((CHUNK, D_MODEL), lambda j: (0, 0)),
            pl.BlockSpec((1, D_MODEL), lambda j: (0, 0)),
            pl.BlockSpec((D_MODEL, tn), lambda j: (0, j)),
        ],
        out_specs=[pl.BlockSpec((1, CHUNK, tn), lambda j: (0, 0, j)),
                   pl.BlockSpec((D_MODEL, tn), lambda j: (0, j))],
        out_shape=[jax.ShapeDtypeStruct((1, CHUNK, IN_COLS), F32),
                   jax.ShapeDtypeStruct((D_MODEL, IN_COLS), BF16)],
        compiler_params=pltpu.CompilerParams(
            dimension_semantics=("arbitrary",),
            vmem_limit_bytes=VMEM_LIMIT),
        name="inproj_meta",
    )(front, g, w_f32)


class _Split:
    def __init__(self, x):
        self.x = x
        self._hi = None
        self._lo = None

    @property
    def hi(self):
        if self._hi is None:
            self._hi = self.x.astype(BF16)
        return self._hi

    @property
    def lo(self):
        if self._lo is None:
            self._lo = (self.x - self.hi.astype(F32)).astype(BF16)
        return self._lo


class _BlockDiag:
    def __init__(self, y):
        self._y = y
        self._hi = None

    @property
    def hi(self):
        if self._hi is None:
            yp = self._y.hi
            lane = lax.broadcasted_iota(jnp.int32, yp.shape, 1)
            first = lane < HEAD
            zero = jnp.zeros_like(yp)
            self._hi = jnp.concatenate([jnp.where(first, yp, zero), jnp.where(first, zero, yp)], axis=0)
        return self._hi


def _mm(terms):
    la = jnp.concatenate([a.hi for a, _ in terms], axis=1)
    lb = jnp.concatenate([b.hi for _, b in terms], axis=0)
    return jnp.dot(la, lb, preferred_element_type=F32)


def _shift_rows(x, carry, n):
    row = lax.broadcasted_iota(jnp.int32, x.shape, 0)
    out = pltpu.roll(x, n, axis=0)
    for i in range(n):
        src = SUBLANES - n + i
        out = jnp.where(row == i, carry[src:src + 1, :], out)
    return out


def _sigmoid(x):
    return 0.5 + 0.5 * jnp.tanh(0.5 * x)


def _silu(x):
    h = 0.5 * x
    return h + h * jnp.tanh(h)


def _interleave(order, stages):
    for name in order:
        if name in stages:
            next(stages[name], None)
    for gen in stages.values():
        for _ in gen:
            pass


_HALF_ORDER = ("prep", "inverse", "prep", "inverse", "prep", "inverse", "prep", "inverse", "inverse", "advance",
               "inverse", "advance", "inverse", "advance", "finish", "inverse", "advance", "finish", "inverse",
               "finish", "inverse", "inverse", "prep")


def _mixer_body(pm_ref, lm_ref, px_ref, lx_ref, mu_ref, vec_ref, wl_ref, ones_ref, tri_ref, o_ref,
                carry_rw, carry_u, state, s_at, s_rt, s_kh, s_bh, s_v, s_pe, s_y,
                r_x, r_tak, r_ark, r_arb, e_bonus, e_gate, e_yb):
    def vec(i):
        return vec_ref[i:i + 1, :]

    w0, a0, k_k, k_a, r_k, lnx_g, lnx_b, cw0, cw1, cw2 = (vec(i) for i in range(10))
    ps = range(N_PAIRS)

    def seg_sums(xs):
        tiles = [x[:, p * LANES:(p + 1) * LANES] for x in xs for p in ps]
        t = jnp.concatenate(tiles, axis=0).astype(BF16)
        s = jnp.dot(t, ones_ref[...], preferred_element_type=F32)
        out = []
        for i in range(len(xs)):
            rows = [s[(i * N_PAIRS + p) * CHUNK:(i * N_PAIRS + p + 1) * CHUNK] for p in ps]
            out.append(jnp.concatenate(rows, axis=1))
        return out

    def prep(p_ref, l_ref, row0, sel):
        p_rw = jnp.concatenate([p_ref[0, row0:row0 + CHUNK, :4 * D_RWKV], l_ref[0, row0:row0 + CHUNK, :]],
                               axis=1)
        prev = _shift_rows(p_rw, carry_rw[...], 1)
        carry_rw[...] = p_rw[CHUNK - SUBLANES:, :]
        pm = p_rw + (prev - p_rw) * mu_ref[...]
        r = pm[:, 0 * D_RWKV:1 * D_RWKV]
        k = pm[:, 1 * D_RWKV:2 * D_RWKV]
        v = pm[:, 2 * D_RWKV:3 * D_RWKV]
        g_r = pm[:, 3 * D_RWKV:4 * D_RWKV]
        lora_in = pm[:, 4 * D_RWKV:]
        lane = lax.broadcasted_iota(jnp.int32, lora_in.shape, 1)
        lora_in = jnp.where(lane < D_LORA, jnp.tanh(lora_in), lora_in).astype(BF16)
        gate = _silu(g_r)

        p_cv = p_ref[0, row0:row0 + CHUNK, 4 * D_RWKV:]
        b_g = p_cv[:, 0 * D_CONV:1 * D_CONV]
        c_g = p_cv[:, 1 * D_CONV:2 * D_CONV]
        h_c = p_cv[:, 2 * D_CONV:3 * D_CONV]
        g_c = p_cv[:, 3 * D_CONV:4 * D_CONV]
        u_c = c_g * h_c
        carry = carry_u[...]
        conv = cw0 * _shift_rows(u_c, carry, 2) + cw1 * _shift_rows(u_c, carry, 1) + cw2 * u_c
        carry_u[...] = u_c[CHUNK - SUBLANES:, :]
        y_b = (b_g * conv * _silu(g_c)).astype(e_yb.dtype)
        yield
        lora = jnp.dot(lora_in, wl_ref[...], preferred_element_type=F32)
        logw = (-math.exp(-0.5)) * _sigmoid(w0 + lora[:, :D_RWKV])
        a = _sigmoid(a0 + lora[:, D_RWKV:])
        kk = k * k_k
        k = k * (1.0 + (a - 1.0) * k_a)
        yield
        logw = _Split(logw)
        cum = jnp.dot(tri_ref[...], jnp.concatenate([logw.hi, logw.lo], axis=0),
                      preferred_element_type=F32)
        logw = logw.x
        e_in = jnp.exp(cum)
        e_out = jnp.exp(-cum)
        e_ex = jnp.exp(cum - logw)
        p_end = e_in[CHUNK - 1:CHUNK, :]
        r_t = r * e_in
        k_h = k * e_out
        yield
        kk_sq, rk_sum = seg_sums([kk * kk, r * k * r_k])
        kk = kk * lax.rsqrt(jnp.maximum(kk_sq, 1e-24))
        a_t = kk * e_ex
        b_h = kk * a * e_out
        bonus = rk_sum * v
        outs = [val.astype(dst.dtype) for dst, val in ((s_at, a_t), (s_rt, r_t), (s_kh, k_h), (s_bh, b_h), (s_v, v))]
        yield
        for dst, val in zip((s_at, s_rt, s_kh, s_bh, s_v), outs):
            for p in ps:
                dst[sel, p] = val[:, p * LANES:(p + 1) * LANES]
        for p in ps:
            s_pe[sel, p] = jnp.broadcast_to(p_end[:, p * LANES:(p + 1) * LANES], (SUBLANES, LANES))
        e_bonus[sel] = bonus
        e_gate[sel] = gate
        e_yb[sel] = y_b

    t_i = lax.broadcasted_iota(jnp.int32, (CHUNK, LANES), 0)
    j_i = lax.broadcasted_iota(jnp.int32, (CHUNK, LANES), 1) & (HEAD - 1)

    def splits(xs):
        return [_Split(x) for x in xs]

    def pmm(xs, ys):
        return [_mm([(x, _BlockDiag(y))]) for x, y in zip(xs, ys)]

    def inverse(sel):
        strict = j_i < t_i
        incl = j_i <= t_i
        eye = (j_i == t_i).astype(F32)

        def same_block(log2_size):
            return (j_i >> log2_size) == (t_i >> log2_size)

        g = []
        for p in ps:
            lhs = jnp.concatenate([s_at[sel, p], s_rt[sel, p]], axis=0)
            rhs_t = jnp.concatenate([_BlockDiag(_Split(s_bh[sel, p])).hi,
                                     _BlockDiag(_Split(s_kh[sel, p])).hi], axis=0)
            g.append(lax.dot_general(lhs, rhs_t, (((1,), (1,)), ((), ())), preferred_element_type=F32))
        t_ab = [jnp.where(strict, g_[:CHUNK, :LANES], 0.0) for g_ in g]
        for p in ps:
            r_arb[sel, p] = jnp.where(incl, g[p][CHUNK:, :LANES], 0.0).astype(r_arb.dtype)
            r_tak[sel, p] = jnp.where(strict, g[p][:CHUNK, LANES:], 0.0).astype(r_tak.dtype)
            r_ark[sel, p] = jnp.where(incl, g[p][CHUNK:, LANES:], 0.0).astype(r_ark.dtype)
        yield

        x = splits([eye - jnp.where(same_block(1), t, 0.0) for t in t_ab])
        for log2_size in range(2, 7):
            level = same_block(log2_size) & jnp.logical_not(same_block(log2_size - 1))
            c = splits([jnp.where(level, t, 0.0) for t in t_ab])
            xc = splits(pmm(x, c))
            yield
            x = splits([x_.x - m for x_, m in zip(x, pmm(xc, x))])
            if log2_size < 6:
                yield
        for p in ps:
            r_x[sel, p] = x[p].hi

    def advance(sel):
        row = lax.broadcasted_iota(jnp.int32, (LANES, LANES), 0)
        col = lax.broadcasted_iota(jnp.int32, (LANES, LANES), 1)
        same_head = (row < HEAD) == (col < HEAD)
        vv = [s_v[sel, p] for p in ps]
        vbd = [_BlockDiag(_Split(v_)) for v_ in vv]
        s0 = [state[p] for p in ps]
        h0 = splits([s.T for s in s0])
        rhs = splits([_mm([(_Split(s_at[sel, p]), h0[p]), (_Split(r_tak[sel, p]), vbd[p])]) for p in ps])
        yield
        u = splits(pmm(splits([r_x[sel, p] for p in ps]), rhs))
        yield
        for p in ps:
            s_y[sel, p] = _mm([(_Split(s_rt[sel, p]), h0[p]), (_Split(r_ark[sel, p]), vbd[p]),
                               (_Split(-r_arb[sel, p]), _BlockDiag(u[p]))])
        yield
        for p in ps:
            wv_t = _Split(jnp.concatenate([vv[p], u[p].x], axis=0).T)
            upd = _mm([(wv_t, _Split(jnp.concatenate([s_kh[sel, p], -s_bh[sel, p]], axis=0)))])
            state[p] = jnp.where(same_head, (s0[p] + upd) * s_pe[sel, p][0:1, :], 0.0)

    def finish(sel, row0):
        y = jnp.concatenate([s_y[sel, p] for p in ps], axis=1)
        inv_n = 1.0 / HEAD
        yield
        mean = seg_sums([y])[0] * inv_n
        yc = y - mean
        yield
        var = seg_sums([yc * yc])[0] * inv_n
        yn = yc * lax.rsqrt(var + LNX_EPS) * lnx_g + lnx_b
        y_a = (yn + e_bonus[sel]) * e_gate[sel]
        o_ref[0, row0:row0 + CHUNK, :D_RWKV] = y_a.astype(o_ref.dtype)
        o_ref[0, row0:row0 + CHUNK, D_RWKV:] = e_yb[sel]

    @pl.when(pl.program_id(1) == 0)
    def _():
        for ref in (carry_rw, carry_u, state, s_at, s_rt, s_kh, s_bh, s_v, s_pe,
                    r_x, r_tak, r_ark, r_arb, e_bonus, e_gate, e_yb):
            ref[...] = jnp.zeros_like(ref)
        for _ in prep(pm_ref, lm_ref, 0, 0):
            pass

    is_drain = pl.program_id(1) == pl.num_programs(1) - 1

    @pl.when(jnp.logical_not(is_drain))
    def _():
        _interleave(_HALF_ORDER, {"advance": advance(1), "finish": finish(1, 0), "inverse": inverse(0),
                                  "prep": prep(px_ref, lx_ref, 0, 1)})
        _interleave(_HALF_ORDER, {"advance": advance(0), "finish": finish(0, CHUNK), "inverse": inverse(1),
                                  "prep": prep(px_ref, lx_ref, CHUNK, 0)})

    @pl.when(is_drain)
    def _():
        _interleave(_HALF_ORDER, {"advance": advance(1), "finish": finish(1, 0), "inverse": inverse(0)})
        _interleave(_HALF_ORDER, {"advance": advance(0), "finish": finish(0, CHUNK)})


def _mixer(proj_meta, lora_meta, proj, lora, mu, vecs, w_lora, seg_ones, tri2):
    bsz, seq, _ = proj.shape
    assert seq % (2 * CHUNK) == 0
    n_steps = seq // (2 * CHUNK) + 1
    pair_buf = pltpu.VMEM((2, N_PAIRS, CHUNK, LANES), F32)
    pair_bf16 = pltpu.VMEM((2, N_PAIRS, CHUNK, LANES), BF16)
    row_buf = pltpu.VMEM((2, CHUNK, D_RWKV), F32)
    const = lambda shape: pl.BlockSpec(shape, lambda b, j: (0,) * len(shape))
    return pl.pallas_call(
        _mixer_body,
        grid=(bsz, n_steps),
        in_specs=[
            const(proj_meta.shape), const(lora_meta.shape),
            pl.BlockSpec((1, 2 * CHUNK, MAIN_COLS), lambda b, j: (b, jnp.minimum(j, n_steps - 2), 0)),
            pl.BlockSpec((1, 2 * CHUNK, LORA_COLS), lambda b, j: (b, jnp.minimum(j, n_steps - 2), 0)),
            const(mu.shape), const(vecs.shape), const(w_lora.shape),
            const(seg_ones.shape), const(tri2.shape),
        ],
        out_specs=pl.BlockSpec((1, 2 * CHUNK, D_MODEL), lambda b, j: (b, jnp.maximum(j - 1, 0), 0)),
        out_shape=jax.ShapeDtypeStruct((bsz, seq, D_MODEL), BF16),
        scratch_shapes=[
            pltpu.VMEM((SUBLANES, RW_COLS), F32),
            pltpu.VMEM((SUBLANES, D_CONV), F32),
            pltpu.VMEM((N_PAIRS, LANES, LANES), F32),
            pair_bf16, pair_bf16, pair_bf16, pair_bf16, pair_buf,
            pltpu.VMEM((2, N_PAIRS, SUBLANES, LANES), F32),
            pair_buf,
            pair_bf16, pair_bf16, pair_bf16, pair_bf16,
            row_buf, row_buf,
            pltpu.VMEM((2, CHUNK, D_CONV), BF16),
        ],
        compiler_params=pltpu.CompilerParams(
            dimension_semantics=("arbitrary", "arbitrary"),
            vmem_limit_bytes=VMEM_LIMIT),
        name="mixer",
    )(proj_meta, lora_meta, proj, lora, mu, vecs, w_lora, seg_ones, tri2)


def _outproj_body(mix_ref, x_ref, w_ref, g_ref, o_ref):
    h = x_ref[...] + jnp.dot(mix_ref[...], w_ref[...], preferred_element_type=F32)
    ms = jnp.mean(h * h, axis=-1, keepdims=True)
    o_ref[...] = h * lax.rsqrt(ms + RMS_EPS) * g_ref[...]


def _outproj(mix2d, x2d, w_bf16, g):
    rows = x2d.shape[0]
    tm = OUTPROJ_TM
    assert rows % tm == 0
    return pl.pallas_call(
        _outproj_body,
        grid=(rows // tm,),
        in_specs=[
            pl.BlockSpec((tm, D_MODEL), lambda i: (i, 0)),
            pl.BlockSpec((tm, D_MODEL), lambda i: (i, 0)),
            pl.BlockSpec((D_MODEL, D_MODEL), lambda i: (0, 0), pipeline_mode=pl.Buffered(1)),
            pl.BlockSpec((1, D_MODEL), lambda i: (0, 0)),
        ],
        out_specs=pl.BlockSpec((tm, D_MODEL), lambda i: (i, 0)),
        out_shape=jax.ShapeDtypeStruct((rows, D_MODEL), F32),
        compiler_params=pltpu.CompilerParams(
            dimension_semantics=("arbitrary",),
            vmem_limit_bytes=PROJ_VMEM_LIMIT),
        name="outproj",
    )(mix2d, x2d, w_bf16, g)


def kernel(x, meta_tokens, norm_in_g, w_in, mu_shift, w0, w_lora_up, a0, a_lora_up, k_k, k_a, r_k, lnx_g, lnx_b, conv_w, w_out, norm_f_g):
    bsz, seq, _ = x.shape
    assert norm_in_g.shape[0] == 1 and seq % CHUNK == 0
    x2d = x.reshape(bsz * seq, D_MODEL)

    front = jnp.concatenate([jnp.zeros((FRONT_PAD, D_MODEL), x.dtype), meta_tokens.astype(x.dtype)], axis=0)
    g_in = norm_in_g[0][None, :]
    meta_all, w_in_bf16 = _inproj_meta(front, g_in, w_in[0])
    proj_meta = jnp.concatenate([meta_all[..., :4 * D_RWKV], meta_all[..., RW_COLS:]], axis=-1)
    lora_meta = meta_all[..., 4 * D_RWKV:RW_COLS]
    proj, lora = _inproj(x, g_in, w_in_bf16)

    w_lora = jnp.zeros((2 * D_LORA, 2 * D_RWKV), F32)
    w_lora = w_lora.at[:D_LORA, :D_RWKV].set(w_lora_up[0]).at[D_LORA:, D_RWKV:].set(a_lora_up[0])
    vecs = jnp.concatenate([
        w0[0][None], a0[0][None], k_k[0][None], k_a[0][None], r_k[0].reshape(1, D_RWKV),
        lnx_g[0][None], lnx_b[0][None], conv_w[0],
        jnp.zeros((16 - 10, D_RWKV), F32)], axis=0)
    lane_head = jnp.arange(LANES) // HEAD
    seg_ones = (lane_head[:, None] == lane_head[None, :]).astype(BF16)
    t_idx = jnp.arange(CHUNK)
    tri = (t_idx[None, :] <= t_idx[:, None]).astype(BF16)
    tri2 = jnp.concatenate([tri, tri], axis=1)

    mix = _mixer(proj_meta, lora_meta, proj, lora, mu_shift, vecs, w_lora.astype(BF16), seg_ones, tri2)

    out = _outproj(mix.reshape(bsz * seq, D_MODEL), x2d, w_out[0].astype(BF16), norm_f_g[None, :])
    return out.reshape(bsz, seq, D_MODEL)
```
